```python
import jax, jax.numpy as jnp
from jax import lax
import numpy as np

D_MODEL = 1024
BATCH = 4
SEQ = 4096
DEPTH = 4

D_PLE = 256
M_HEADS = 4
M_HEAD_DIM = 128
M_WIDTH = M_HEADS * M_HEAD_DIM
M_CONV = 5
M_CHUNK = 64
N_GATE = 4 * M_HEADS
A_HEADS = 8
A_KV_HEADS = 2
A_HEAD_DIM = 64
A_WIDTH = A_HEADS * A_HEAD_DIM
A_KV_WIDTH = A_KV_HEADS * A_HEAD_DIM
A_GROUP = A_HEADS // A_KV_HEADS
WINDOW = 128
A_BLOCK = 128
MIX_WIDTH = M_WIDTH + A_WIDTH
IN_WIDTH = 4 * M_WIDTH + N_GATE + A_WIDTH + 2 * A_KV_WIDTH
D_FF = ((-(-8 * D_MODEL // 3) + 255) // 256) * 256
ALPHA = (2 * DEPTH) ** 0.25
BETA = (8 * DEPTH) ** -0.25
LN_EPS = 1e-5
MH_EPS = 1e-6

kernel_name = "hybrid_mlstm_swa_deepnorm_encoder"


def _layer_norm(x, g, b):
    xf = x.astype(jnp.float32)
    mu = xf.mean(-1, keepdims=True)
    var = jnp.mean(jnp.square(xf - mu), -1, keepdims=True)
    return ((xf - mu) * lax.rsqrt(var + LN_EPS) * g + b).astype(x.dtype)


def _centred_dwconv(x, w, b):
    pad = w.shape[0] // 2
    y = lax.conv_general_dilated(x, w[:, None, :], window_strides=(1,), padding=[(pad, pad)],
                                 dimension_numbers=("NWC", "WIO", "NWC"),
                                 feature_group_count=x.shape[-1])
    return y + b


def _mlstm_scan(q, k, v, i_pre, log_f):
    B, H, S, dh = q.shape
    L = M_CHUNK
    nc = S // L

    def chunks(t):
        return jnp.moveaxis(t.reshape(B, H, nc, L, *t.shape[3:]), 2, 0)

    lower = jnp.tril(jnp.ones((L, L), dtype=bool))

    def step(carry, xs):
        C, n, m = carry
        qc, kc, vc, ic, fc = xs
        b = jnp.cumsum(fc, axis=-1)
        dmat = b[..., :, None] - b[..., None, :] + ic[..., None, :]
        dmat = jnp.where(lower, dmat, -jnp.inf)
        inter = b + m[..., None]
        m_t = jnp.maximum(dmat.max(-1), inter)
        w_intra = jnp.exp(dmat - m_t[..., None])
        w_inter = jnp.exp(inter - m_t)
        s = jnp.einsum('bhtd,bhsd->bhts', qc, kc) * w_intra
        num = jnp.einsum('bhts,bhse->bhte', s, vc) + w_inter[..., None] * jnp.einsum('bhed,bhtd->bhte', C, qc)
        den = s.sum(-1) + w_inter * jnp.einsum('bhd,bhtd->bht', n, qc)
        h = num / jnp.maximum(jnp.abs(den), jnp.exp(-m_t))[..., None]
        g = b[..., -1]
        a = g[..., None] - b + ic
        m_new = jnp.maximum(g + m, a.max(-1))
        decay = jnp.exp(g + m - m_new)
        wa = jnp.exp(a - m_new[..., None])
        C = decay[..., None, None] * C + jnp.einsum('bhs,bhse,bhsd->bhed', wa, vc, kc)
        n = decay[..., None] * n + jnp.einsum('bhs,bhsd->bhd', wa, kc)
        return (C, n, m_new), h

    init = (jnp.zeros((B, H, dh, dh), jnp.float32), jnp.zeros((B, H, dh), jnp.float32),
            jnp.zeros((B, H), jnp.float32))
    _, h = lax.scan(step, init, (chunks(q), chunks(k), chunks(v), chunks(i_pre), chunks(log_f)))
    return jnp.moveaxis(h, 0, 2).reshape(B, H, S, dh)


def _mlstm_group(qk, v, o, gates, b_gate, conv_w, conv_b, norm_g):
    B, S, _ = v.shape
    qk = jax.nn.silu(_centred_dwconv(qk, conv_w, conv_b))
    q, k = jnp.split(qk, 2, axis=-1)

    def to_heads(t):
        return t.reshape(B, S, M_HEADS, M_HEAD_DIM).transpose(0, 2, 1, 3).astype(jnp.float32)

    q, k, vh = to_heads(q), to_heads(k) * (M_HEAD_DIM ** -0.5), to_heads(v)
    g = (gates + b_gate).astype(jnp.float32).reshape(B, S, 4, M_HEADS).transpose(2, 0, 3, 1)
    h_fwd = _mlstm_scan(q, k, vh, g[0], jax.nn.log_sigmoid(g[2]))

    def flip(t):
        return jnp.flip(t, axis=2)

    h_bwd = flip(_mlstm_scan(flip(q), flip(k), flip(vh), flip(g[1]), flip(jax.nn.log_sigmoid(g[3]))))
    h = h_fwd + h_bwd
    mu = h.mean(-1, keepdims=True)
    var = jnp.mean(jnp.square(h - mu), -1, keepdims=True)
    h = ((h - mu) * lax.rsqrt(var + MH_EPS)).transpose(0, 2, 1, 3).reshape(B, S, M_WIDTH)
    return (h * norm_g * jax.nn.sigmoid(o.astype(jnp.float32))).astype(v.dtype)


def _window_gqa(q, k, v, sink):
    B, S, _ = q.shape
    nb = S // A_BLOCK
    qb = q.reshape(B, nb, A_BLOCK, A_KV_HEADS, A_GROUP, A_HEAD_DIM)

    def band(t):
        t = t.reshape(B, nb, A_BLOCK, A_KV_HEADS, A_HEAD_DIM)
        t = jnp.pad(t, ((0, 0), (1, 1), (0, 0), (0, 0), (0, 0)))
        return jnp.concatenate([t[:, :-2], t[:, 1:-1], t[:, 2:]], axis=2)

    kb, vb = band(k), band(v)
    s = jnp.einsum('bnqhgd,bnkhd->bnhgqk', qb, kb).astype(jnp.float32) * (A_HEAD_DIM ** -0.5)
    blk = jnp.arange(nb)[:, None]
    qpos = blk * A_BLOCK + jnp.arange(A_BLOCK)[None]
    kpos = (blk - 1) * A_BLOCK + jnp.arange(3 * A_BLOCK)[None]
    rel = jnp.abs(kpos[:, None, :] - qpos[:, :, None])
    valid = (rel <= WINDOW) & (kpos[:, None, :] >= 0) & (kpos[:, None, :] < S)
    slopes = jnp.exp2(-8.0 * jnp.arange(1, A_HEADS + 1, dtype=jnp.float32) / A_HEADS)
    bias = -slopes.reshape(A_KV_HEADS, A_GROUP)[None, :, :, None, None] * rel[:, None, None].astype(jnp.float32)
    s = jnp.where(valid[:, None, None], s + bias, -jnp.inf)
    sk = sink.astype(jnp.float32).reshape(A_KV_HEADS, A_GROUP)[None, None, :, :, None, None]
    mx = jnp.maximum(s.max(-1, keepdims=True), sk)
    e = jnp.exp(s - mx)
    prob = e / (e.sum(-1, keepdims=True) + jnp.exp(sk - mx))
    out = jnp.einsum('bnhgqk,bnkhd->bnqhgd', prob.astype(v.dtype), vb)
    return out.reshape(B, S, A_WIDTH)


def _hybrid_layer(x, p_i, w_in, b_gate, conv_w, conv_b, mlstm_norm_g, attn_sink, w_out,
                  ln1_g, ln1_b, w_ffn_in, w_ffn_out, ln2_g, ln2_b, w_ple_gate, w_ple_proj):
    u = x @ w_in
    cuts = np.cumsum([2 * M_WIDTH, M_WIDTH, M_WIDTH, N_GATE, A_WIDTH, A_KV_WIDTH]).tolist()
    m_qk, m_v, m_o, m_gates, a_q, a_k, a_v = jnp.split(u, cuts, axis=-1)
    h_m = _mlstm_group(m_qk, m_v, m_o, m_gates, b_gate, conv_w, conv_b, mlstm_norm_g)
    h_a = _window_gqa(a_q, a_k, a_v, attn_sink)
    mix = jnp.concatenate([h_m, h_a], axis=-1) @ w_out
    x = _layer_norm(ALPHA * x + mix, ln1_g, ln1_b)
    gate_ff, up_ff = jnp.split(x @ w_ffn_in, 2, axis=-1)
    ffn = (jax.nn.silu(gate_ff) * up_ff) @ w_ffn_out
    ple = jax.nn.sigmoid(x @ w_ple_gate) * (p_i @ w_ple_proj)
    return _layer_norm(ALPHA * x + ffn + ple, ln2_g, ln2_b)


def setup_inputs(seed: int = 0) -> dict:
    key = jax.random.key(seed)
    ks = jax.random.split(key, 20)
    nrm = lambda k, shape: jax.random.normal(k, shape, jnp.float32)
    col_scale = np.ones((IN_WIDTH,), np.float32)
    col_scale[3 * M_WIDTH - M_WIDTH:3 * M_WIDTH] = BETA
    av0 = 4 * M_WIDTH + N_GATE + A_WIDTH + A_KV_WIDTH
    col_scale[av0:av0 + A_KV_WIDTH] = BETA
    w_in = nrm(ks[2], (DEPTH, D_MODEL, IN_WIDTH)) * (D_MODEL ** -0.5) * jnp.asarray(col_scale)
    f_lin = jnp.linspace(3.0, 6.0, M_HEADS, dtype=jnp.float32)
    b_gate = jnp.concatenate([0.1 * nrm(ks[3], (DEPTH, 2 * M_HEADS)),
                              jnp.concatenate([f_lin, f_lin])[None] + 0.1 * nrm(ks[4], (DEPTH, 2 * M_HEADS))], axis=-1)
    return {
        "x": nrm(ks[0], (BATCH, SEQ, D_MODEL)),
        "p": nrm(ks[1], (DEPTH, BATCH, SEQ, D_PLE)),
        "w_in": w_in,
        "b_gate": b_gate,
        "conv_w": nrm(ks[5], (DEPTH, M_CONV, 2 * M_WIDTH)) * (M_CONV ** -0.5),
        "conv_b": 0.01 * nrm(ks[6], (DEPTH, 2 * M_WIDTH)),
        "mlstm_norm_g": 1.0 + 0.01 * nrm(ks[7], (DEPTH, M_WIDTH)),
        "attn_sink": 0.5 * nrm(ks[8], (DEPTH, A_HEADS)),
        "w_out": nrm(ks[9], (DEPTH, MIX_WIDTH, D_MODEL)) * (MIX_WIDTH ** -0.5) * BETA,
        "ln1_g": 1.0 + 0.01 * nrm(ks[10], (DEPTH, D_MODEL)),
        "ln1_b": 0.01 * nrm(ks[11], (DEPTH, D_MODEL)),
        "w_ffn_in": nrm(ks[12], (DEPTH, D_MODEL, 2 * D_FF)) * (D_MODEL ** -0.5) * BETA,
        "w_ffn_out": nrm(ks[13], (DEPTH, D_FF, D_MODEL)) * (D_FF ** -0.5) * BETA,
        "ln2_g": 1.0 + 0.01 * nrm(ks[14], (DEPTH, D_MODEL)),
        "ln2_b": 0.01 * nrm(ks[15], (DEPTH, D_MODEL)),
        "w_ple_gate": nrm(ks[16], (DEPTH, D_MODEL, D_MODEL)) * (D_MODEL ** -0.5),
        "w_ple_proj": nrm(ks[17], (DEPTH, D_PLE, D_MODEL)) * (D_PLE ** -0.5) * BETA,
    }


def reference(x, p, w_in, b_gate, conv_w, conv_b, mlstm_norm_g, attn_sink, w_out,
              ln1_g, ln1_b, w_ffn_in, w_ffn_out, ln2_g, ln2_b, w_ple_gate, w_ple_proj):
    h = x
    for i in range(DEPTH):
        h = _hybrid_layer(h, p[i], w_in[i], b_gate[i], conv_w[i], conv_b[i], mlstm_norm_g[i],
                          attn_sink[i], w_out[i], ln1_g[i], ln1_b[i], w_ffn_in[i], w_ffn_out[i],
                          ln2_g[i], ln2_b[i], w_ple_gate[i], w_ple_proj[i])
    return h
```

```python
import functools

import numpy as np
import jax
import jax.numpy as jnp
from jax import lax
from jax.experimental import pallas as pl
from jax.experimental.pallas import tpu as pltpu

F32 = jnp.float32
BF16 = jnp.bfloat16

M_HEADS = 4
M_HEAD_DIM = 128
M_WIDTH = M_HEADS * M_HEAD_DIM
M_CONV = 5
N_GATE = 4 * M_HEADS
A_HEADS = 8
A_KV_HEADS = 2
A_HEAD_DIM = 64
A_GROUP = A_HEADS // A_KV_HEADS
A_WIDTH = A_HEADS * A_HEAD_DIM
A_KV_WIDTH = A_KV_HEADS * A_HEAD_DIM
WINDOW = 128
LN_EPS = 1e-5
MH_EPS = 1e-6

LANES = 128
BF16_SUBLANES = 16

CHUNK = 128
A_BLOCK = 128
U_WIDTH = 4 * M_WIDTH + A_WIDTH + 2 * A_KV_WIDTH
G_WIDTH = LANES
U_Q, U_K, U_V, U_O = 0, M_WIDTH, 2 * M_WIDTH, 3 * M_WIDTH
U_AQ = 4 * M_WIDTH
U_AKV = U_AQ + A_WIDTH
INPROJ_TM = 1024
POST_TM = 512
VMEM_LIMIT = 56 * 1024 * 1024


def _sigmoid(x):
    return 1.0 / (1.0 + jnp.exp(-x))


def _layer_norm(z, g, b):
    mu = jnp.mean(z, axis=-1, keepdims=True)
    zc = z - mu
    var = jnp.mean(zc * zc, axis=-1, keepdims=True)
    return zc * lax.rsqrt(var + LN_EPS) * g + b


def _const_spec(shape):
    return pl.BlockSpec(shape, lambda *_: (0,) * len(shape), pipeline_mode=pl.Buffered(1))


def _inproj_kernel(x_ref, w_ref, u_ref, g_ref):
    xb = x_ref[...].astype(BF16)
    for c0 in range(0, U_WIDTH, 512):
        c1 = min(c0 + 512, U_WIDTH)
        u_ref[:, c0:c1] = jnp.dot(xb, w_ref[:, c0:c1], preferred_element_type=F32).astype(BF16)
    g_ref[...] = jnp.dot(xb, w_ref[:, U_WIDTH:], preferred_element_type=F32)


def _inproj(x2, w):
    t, d = x2.shape
    tm = INPROJ_TM
    return pl.pallas_call(
        _inproj_kernel,
        grid=(t // tm,),
        in_specs=[pl.BlockSpec((tm, d), lambda i: (i, 0)),
                  _const_spec((d, U_WIDTH + G_WIDTH))],
        out_specs=[pl.BlockSpec((tm, U_WIDTH), lambda i: (i, 0)),
                   pl.BlockSpec((tm, G_WIDTH), lambda i: (i, 0))],
        out_shape=[jax.ShapeDtypeStruct((t, U_WIDTH), BF16),
                   jax.ShapeDtypeStruct((t, G_WIDTH), F32)],
        compiler_params=pltpu.CompilerParams(dimension_semantics=("arbitrary",),
                                             vmem_limit_bytes=VMEM_LIMIT),
        name="inproj",
    )(x2, w)


def _gateprep_kernel(g_ref, bias_ref, rows_ref, gt_ref):
    s = g_ref.shape[0]

    def transpose_tile(i, carry):
        r0 = pl.multiple_of(i * LANES, LANES)
        tile = g_ref[pl.ds(r0, LANES), :] + bias_ref[...]
        gt_ref[:, pl.ds(r0, LANES)] = tile.T[:N_GATE, :]
        return carry

    lax.fori_loop(0, s // LANES, transpose_tile, 0)
    gt = gt_ref[...]
    nd = 2 * M_HEADS
    ig = gt[0:nd]
    fg = gt[nd:2 * nd]
    lf = jnp.minimum(fg, 0.0) - jnp.log1p(jnp.exp(-jnp.abs(fg)))
    pos = lax.broadcasted_iota(jnp.int32, (nd, s), 1) & (CHUNK - 1)
    pre, suf = lf, lf
    sh = 1
    while sh < CHUNK:
        pre = pre + jnp.where(pos >= sh, pltpu.roll(pre, sh, 1), 0.0)
        suf = suf + jnp.where(pos < CHUNK - sh, pltpu.roll(suf, s - sh, 1), 0.0)
        sh *= 2
    row = lax.broadcasted_iota(jnp.int32, (nd, s), 0)
    r = ig - jnp.where(row < M_HEADS, pre, suf)
    pad = jnp.zeros((4, s), F32)
    for h in range(M_HEADS):
        hb = M_HEADS + h
        rows_ref[h] = jnp.concatenate(
            [r[h:h + 1], r[hb:hb + 1], lf[h:h + 1], lf[hb:hb + 1], pad], axis=0)


def _gateprep(g2, bias, batch, seq):
    return pl.pallas_call(
        _gateprep_kernel,
        grid=(batch,),
        in_specs=[pl.BlockSpec((seq, G_WIDTH), lambda b: (b, 0)),
                  pl.BlockSpec((1, G_WIDTH), lambda b: (0, 0))],
        out_specs=pl.BlockSpec((None, M_HEADS, 8, seq), lambda b: (b, 0, 0, 0)),
        out_shape=jax.ShapeDtypeStruct((batch, M_HEADS, 8, seq), F32),
        scratch_shapes=[pltpu.VMEM((N_GATE, seq), F32)],
        compiler_params=pltpu.CompilerParams(dimension_semantics=("arbitrary",),
                                             vmem_limit_bytes=VMEM_LIMIT),
        name="gateprep",
    )(g2, bias)


def _conv_silu(x_ref, w_ref, b_ref, r0, seq):
    halo = BF16_SUBLANES
    cur = x_ref[pl.ds(r0, CHUNK), :].astype(F32)
    ps = pl.multiple_of(jnp.maximum(r0 - halo, 0), halo)
    ns = pl.multiple_of(jnp.minimum(r0 + CHUNK, seq - halo), halo)
    prev = jnp.where(r0 > 0, x_ref[pl.ds(ps, halo), :].astype(F32), 0.0)
    nxt = jnp.where(r0 + CHUNK < seq, x_ref[pl.ds(ns, halo), :].astype(F32), 0.0)
    win = jnp.concatenate([prev, cur, nxt], axis=0)
    n = CHUNK + 2 * halo
    w = w_ref[...]
    y = b_ref[...] + jnp.zeros((CHUNK, LANES), F32)
    for k in range(M_CONV):
        shift = (M_CONV // 2 - k) % n
        rolled = win if shift == 0 else pltpu.roll(win, shift, 0)
        y = y + w[k:k + 1] * rolled[halo:halo + CHUNK]
    return y * _sigmoid(y)


def _mlstm_kernel(q_ref, k_ref, v_ref, o_ref, rows_ref, cwq_ref, cbq_ref, cwk_ref, cbk_ref,
                  ng_ref, out_ref, qs_ref, kt_ref, hf_ref, hb_ref, cf_ref, cb_ref):
    seq = q_ref.shape[0]
    nc = seq // CHUNK
    dh = M_HEAD_DIM

    def conv_body(i, carry):
        r0 = pl.multiple_of(i * CHUNK, CHUNK)
        qs_ref[pl.ds(r0, CHUNK), :] = _conv_silu(q_ref, cwq_ref, cbq_ref, r0, seq).astype(BF16)
        kc = _conv_silu(k_ref, cwk_ref, cbk_ref, r0, seq) * (dh ** -0.5)
        kt_ref[:, pl.ds(r0, CHUNK)] = kc.T.astype(BF16)
        return carry

    lax.fori_loop(0, nc, conv_body, 0)

    t_idx = lax.broadcasted_iota(jnp.int32, (CHUNK, CHUNK), 0)
    s_idx = lax.broadcasted_iota(jnp.int32, (CHUNK, CHUNK), 1)
    tri_f = s_idx <= t_idx
    tri_b = s_idx >= t_idx
    ones_col = (lax.broadcasted_iota(jnp.int32, (CHUNK, LANES), 1) == 0).astype(BF16)

    def step(ci, rev, c_ref, m, h_ref):
        r0 = pl.multiple_of(ci * CHUNK, CHUNK)
        tri = tri_b if rev else tri_f
        qc = qs_ref[pl.ds(r0, CHUNK), :]
        ktc = kt_ref[:, pl.ds(r0, CHUNK)]
        vext = jnp.concatenate([v_ref[pl.ds(r0, CHUNK), :], ones_col], axis=1)
        r_row = rows_ref[int(rev):int(rev) + 1, pl.ds(r0, CHUNK)]
        lf_row = rows_ref[2 + int(rev):3 + int(rev), pl.ds(r0, CHUNK)]
        rm = jnp.where(tri, r_row, -jnp.inf)
        mu = jnp.maximum(jnp.max(rm, axis=1, keepdims=True), m)
        w_intra = jnp.exp(rm - mu)
        w_inter = jnp.exp(m - mu)
        b_col = jnp.sum(jnp.where(tri, lf_row, 0.0), axis=1, keepdims=True)
        sc = jnp.dot(qc, ktc, preferred_element_type=F32) * w_intra
        tot = jnp.dot(sc.astype(BF16), vext, preferred_element_type=F32)
        tot = tot + w_inter * jnp.dot(qc, c_ref[...].astype(BF16), preferred_element_type=F32)
        den = tot[:, dh:dh + 1]
        h_ref[pl.ds(r0, CHUNK), :] = tot[:, :dh] / jnp.maximum(jnp.abs(den), jnp.exp(-(b_col + mu)))
        g = jnp.sum(lf_row, axis=1, keepdims=True)
        mm = jnp.maximum(m, jnp.max(r_row, axis=1, keepdims=True))
        kw = (ktc.astype(F32) * jnp.exp(r_row - mm)).astype(BF16)
        c_ref[...] = jnp.exp(m - mm) * c_ref[...] + jnp.dot(kw, vext, preferred_element_type=F32)
        return g + mm

    cf_ref[...] = jnp.zeros_like(cf_ref)
    cb_ref[...] = jnp.zeros_like(cb_ref)

    def chunk_body(c, carry):
        mf, mb = carry
        mf = step(c, False, cf_ref, mf, hf_ref)
        mb = step(nc - 1 - c, True, cb_ref, mb, hb_ref)
        return mf, mb

    zero = jnp.zeros((1, 1), F32)
    lax.fori_loop(0, nc, chunk_body, (zero, zero))

    def norm_body(i, carry):
        r0 = pl.multiple_of(i * CHUNK, CHUNK)
        h = hf_ref[pl.ds(r0, CHUNK), :] + hb_ref[pl.ds(r0, CHUNK), :]
        mu = jnp.mean(h, axis=-1, keepdims=True)
        hc = h - mu
        var = jnp.mean(hc * hc, axis=-1, keepdims=True)
        hn = hc * lax.rsqrt(var + MH_EPS)
        og = _sigmoid(o_ref[pl.ds(r0, CHUNK), :].astype(F32))
        out_ref[pl.ds(r0, CHUNK), :] = (hn * ng_ref[...] * og).astype(BF16)
        return carry

    lax.fori_loop(0, nc, norm_body, 0)


def _mlstm(u, rows, conv_w, conv_b, norm_g, batch, seq):
    dh = M_HEAD_DIM
    nh = M_HEADS

    def ucol(base):
        return pl.BlockSpec((seq, dh), lambda b, h: (b, base // dh + h))

    return pl.pallas_call(
        _mlstm_kernel,
        grid=(batch, nh),
        in_specs=[ucol(U_Q), ucol(U_K), ucol(U_V), ucol(U_O),
                  pl.BlockSpec((None, None, 8, seq), lambda b, h: (b, h, 0, 0)),
                  pl.BlockSpec((M_CONV, dh), lambda b, h: (0, h)),
                  pl.BlockSpec((1, dh), lambda b, h: (0, h)),
                  pl.BlockSpec((M_CONV, dh), lambda b, h: (0, nh + h)),
                  pl.BlockSpec((1, dh), lambda b, h: (0, nh + h)),
                  pl.BlockSpec((1, dh), lambda b, h: (0, h))],
        out_specs=pl.BlockSpec((seq, dh), lambda b, h: (b, h)),
        out_shape=jax.ShapeDtypeStruct((batch * seq, M_WIDTH), BF16),
        scratch_shapes=[pltpu.VMEM((seq, dh), BF16),
                        pltpu.VMEM((dh, seq), BF16),
                        pltpu.VMEM((seq, dh), F32),
                        pltpu.VMEM((seq, dh), F32),
                        pltpu.VMEM((dh, 2 * dh), F32),
                        pltpu.VMEM((dh, 2 * dh), F32)],
        compiler_params=pltpu.CompilerParams(dimension_semantics=("arbitrary", "arbitrary"),
                                             vmem_limit_bytes=VMEM_LIMIT),
        name="mlstm",
    )(u, u, u, u, rows, conv_w, conv_b, conv_w, conv_b, norm_g)


def _attn_kernel(sink_ref, q_ref, kv_ref, out_ref):
    seq = q_ref.shape[0]
    nb = seq // A_BLOCK
    span = 3 * A_BLOCK
    lo = lax.broadcasted_iota(jnp.int32, (A_BLOCK, LANES), 1) < A_HEAD_DIM
    qi = lax.broadcasted_iota(jnp.int32, (A_BLOCK, span), 0)
    ki = lax.broadcasted_iota(jnp.int32, (A_BLOCK, span), 1)

    def block(n, carry):
        q0 = pl.multiple_of(n * A_BLOCK, A_BLOCK)
        k0 = pl.multiple_of(jnp.clip(q0 - A_BLOCK, 0, seq - span), A_BLOCK)
        k3 = kv_ref[pl.ds(k0, span), 0:A_KV_WIDTH]
        v3 = kv_ref[pl.ds(k0, span), A_KV_WIDTH:2 * A_KV_WIDTH]
        rel = jnp.abs(ki - qi + (k0 - q0))
        valid = rel <= WINDOW
        relf = rel.astype(F32)
        for j in range(A_GROUP):
            qp = q_ref[pl.ds(q0, A_BLOCK), j * LANES:(j + 1) * LANES]
            halves = []
            for half in range(A_KV_HEADS):
                head = j + A_GROUP * half
                qm = jnp.where(lo if half == 0 else jnp.logical_not(lo), qp, jnp.zeros_like(qp))
                sc = lax.dot_general(qm, k3, (((1,), (1,)), ((), ())), preferred_element_type=F32)
                sc = sc * (A_HEAD_DIM ** -0.5) - (2.0 ** -(head + 1)) * relf
                sc = jnp.where(valid, sc, -jnp.inf)
                sk = sink_ref[head]
                mx = jnp.maximum(jnp.max(sc, axis=1, keepdims=True), sk)
                e = jnp.exp(sc - mx)
                den = jnp.sum(e, axis=1, keepdims=True) + jnp.exp(sk - mx)
                pv = jnp.dot(e.astype(BF16), v3, preferred_element_type=F32)
                halves.append(pv / den)
            out_ref[pl.ds(q0, A_BLOCK), j * LANES:(j + 1) * LANES] = (
                jnp.where(lo, halves[0], halves[1]).astype(BF16))
        return carry

    lax.fori_loop(0, nb, block, 0)


def _attn(u, sink, batch, seq):
    return pl.pallas_call(
        _attn_kernel,
        grid=(batch,),
        in_specs=[pl.BlockSpec(memory_space=pltpu.SMEM),
                  pl.BlockSpec((seq, A_WIDTH), lambda b: (b, U_AQ // A_WIDTH)),
                  pl.BlockSpec((seq, 2 * A_KV_WIDTH), lambda b: (b, U_AKV // (2 * A_KV_WIDTH)))],
        out_specs=pl.BlockSpec((seq, A_WIDTH), lambda b: (b, 0)),
        out_shape=jax.ShapeDtypeStruct((batch * seq, A_WIDTH), BF16),
        compiler_params=pltpu.CompilerParams(dimension_semantics=("arbitrary",),
                                             vmem_limit_bytes=VMEM_LIMIT),
        name="attn",
    )(sink, u, u)


def _ff_chunks(d_ff):
    step = 1024
    return [(c0, min(c0 + step, d_ff)) for c0 in range(0, d_ff, step)]


def _post_kernel(alpha, x_ref, hm_ref, ha_ref, p_ref, wom_ref, woa_ref, wg_ref, wu_ref, wd_ref,
                 wpg_ref, wpp_ref, ln1g_ref, ln1b_ref, ln2g_ref, ln2b_ref, y_ref):
    mix = jnp.dot(hm_ref[...], wom_ref[...], preferred_element_type=F32)
    mix = mix + jnp.dot(ha_ref[...], woa_ref[...], preferred_element_type=F32)
    x1 = _layer_norm(alpha * x_ref[...] + mix, ln1g_ref[...], ln1b_ref[...])
    x1b = x1.astype(BF16)
    ple_gate = _sigmoid(jnp.dot(x1b, wpg_ref[...], preferred_element_type=F32))
    acc = alpha * x1 + ple_gate * jnp.dot(p_ref[...].astype(BF16), wpp_ref[...],
                                          preferred_element_type=F32)
    for c0, c1 in _ff_chunks(wd_ref.shape[0]):
        gate = jnp.dot(x1b, wg_ref[:, c0:c1], preferred_element_type=F32)
        up = jnp.dot(x1b, wu_ref[:, c0:c1], preferred_element_type=F32)
        hid = (gate * _sigmoid(gate) * up).astype(BF16)
        acc = acc + jnp.dot(hid, wd_ref[c0:c1, :], preferred_element_type=F32)
    y_ref[...] = _layer_norm(acc, ln2g_ref[...], ln2b_ref[...])


def _post(alpha, x2, hm, ha, p2, wom, woa, wg, wu, wd, wpg, wpp, ln1g, ln1b, ln2g, ln2b):
    t, d = x2.shape
    tm = POST_TM

    def rows(width):
        return pl.BlockSpec((tm, width), lambda i: (i, 0))

    weights = [wom, woa, wg, wu, wd, wpg, wpp, ln1g, ln1b, ln2g, ln2b]
    return pl.pallas_call(
        functools.partial(_post_kernel, alpha),
        grid=(t // tm,),
        in_specs=[rows(d), rows(hm.shape[1]), rows(ha.shape[1]), rows(p2.shape[1])]
                 + [_const_spec(w.shape) for w in weights],
        out_specs=rows(d),
        out_shape=jax.ShapeDtypeStruct((t, d), F32),
        compiler_params=pltpu.CompilerParams(dimension_semantics=("arbitrary",),
                                             vmem_limit_bytes=VMEM_LIMIT),
        name="post",
    )(x2, hm, ha, p2, *weights)


def _attn_head_perm():
    cols = []
    for j in range(A_GROUP):
        for half in range(A_KV_HEADS):
            head = j + A_GROUP * half
            cols.extend(range(head * A_HEAD_DIM, (head + 1) * A_HEAD_DIM))
    return np.asarray(cols, np.int32)


def kernel(x, p, w_in, b_gate, conv_w, conv_b, mlstm_norm_g, attn_sink, w_out, ln1_g, ln1_b,
           w_ffn_in, w_ffn_out, ln2_g, ln2_b, w_ple_gate, w_ple_proj):
    batch, seq, d = x.shape
    depth = w_in.shape[0]
    d_ff = w_ffn_out.shape[1]
    t = batch * seq
    alpha = float((2 * depth) ** 0.25)
    assert seq % CHUNK == 0 and seq % A_BLOCK == 0 and seq >= 3 * A_BLOCK
    assert t % INPROJ_TM == 0 and t % POST_TM == 0
    assert w_in.shape[2] == U_WIDTH + N_GATE

    perm = _attn_head_perm()
    g0 = 4 * M_WIDTH
    aq0 = g0 + N_GATE
    akv0 = aq0 + A_WIDTH
    w_gate = jnp.pad(w_in[:, :, g0:aq0], ((0, 0), (0, 0), (0, G_WIDTH - N_GATE)))
    w_in_k = jnp.concatenate(
        [w_in[:, :, :g0], w_in[:, :, aq0 + perm], w_in[:, :, akv0:], w_gate], axis=2).astype(BF16)
    gate_bias = jnp.pad(b_gate, ((0, 0), (0, G_WIDTH - N_GATE)))[:, None, :]
    wom = w_out[:, :M_WIDTH].astype(BF16)
    woa = w_out[:, M_WIDTH + perm].astype(BF16)
    wg = w_ffn_in[:, :, :d_ff].astype(BF16)
    wu = w_ffn_in[:, :, d_ff:].astype(BF16)
    wd = w_ffn_out.astype(BF16)
    wpg = w_ple_gate.astype(BF16)
    wpp = w_ple_proj.astype(BF16)

    h = x.reshape(t, d)
    p2 = p.reshape(depth, t, p.shape[-1])
    for i in range(depth):
        u, g = _inproj(h, w_in_k[i])
        rows = _gateprep(g, gate_bias[i], batch, seq)
        hm = _mlstm(u, rows, conv_w[i], conv_b[i][None], mlstm_norm_g[i][None], batch, seq)
        ha = _attn(u, attn_sink[i], batch, seq)
        h = _post(alpha, h, hm, ha, p2[i], wom[i], woa[i], wg[i], wu[i], wd[i], wpg[i], wpp[i],
                  ln1_g[i][None], ln1_b[i][None], ln2_g[i][None], ln2_b[i][None])
    return h.reshape(batch, seq, d)
```

```python
import functools

import jax
import jax.numpy as jnp
from jax import lax
from jax.experimental import pallas as pl
from jax.experimental.pallas import tpu as pltpu

F32 = jnp.float32
BF16 = jnp.bfloat16

M_HEADS = 4
M_HEAD_DIM = 128
M_WIDTH = M_HEADS * M_HEAD_DIM
M_CONV = 5
N_GATE = 4 * M_HEADS
A_HEADS = 8
A_KV_HEADS = 2
A_HEAD_DIM = 64
A_GROUP = A_HEADS // A_KV_HEADS
A_WIDTH = A_HEADS * A_HEAD_DIM
A_KV_WIDTH = A_KV_HEADS * A_HEAD_DIM
WINDOW = 128
LN_EPS = 1e-5
MH_EPS = 1e-6

LANES = 128
BF16_SUBLANES = 16

CHUNK = 128
A_BLOCK = 128
U_MAIN = 4 * M_WIDTH
U_WIDTH = U_MAIN + A_WIDTH + 2 * A_KV_WIDTH
G_WIDTH = LANES
U_Q, U_K, U_V, U_O = 0, M_WIDTH, 2 * M_WIDTH, 3 * M_WIDTH
U_AQ = U_MAIN
U_AKV = U_AQ + A_WIDTH
GATE_ROWS = 16
INPROJ_TM = 1024
POST_TM = 512
VMEM_LIMIT = 56 * 1024 * 1024


def _sigmoid(x):
    return 1.0 / (1.0 + jnp.exp(-x))


def _layer_norm(z, g, b):
    mu = jnp.mean(z, axis=-1, keepdims=True)
    zc = z - mu
    var = jnp.mean(zc * zc, axis=-1, keepdims=True)
    return zc * lax.rsqrt(var + LN_EPS) * g + b


def _const_spec(shape):
    return pl.BlockSpec(shape, lambda *_: (0,) * len(shape), pipeline_mode=pl.Buffered(1))


def _inproj_kernel(x_ref, wm_ref, waq_ref, wakv_ref, wg_ref, u_ref, g_ref):
    xb = x_ref[...].astype(BF16)
    for c0 in range(0, U_MAIN, 512):
        u_ref[:, c0:c0 + 512] = jnp.dot(
            xb, wm_ref[:, c0:c0 + 512], preferred_element_type=F32).astype(BF16)
    u_ref[:, U_AQ:U_AKV] = jnp.dot(xb, waq_ref[...], preferred_element_type=F32).astype(BF16)
    u_ref[:, U_AKV:] = jnp.dot(xb, wakv_ref[...], preferred_element_type=F32).astype(BF16)
    g_ref[...] = jnp.dot(xb, wg_ref[...], preferred_element_type=F32)


def _inproj(x2, wm, waq, wakv, wg):
    t, d = x2.shape
    tm = INPROJ_TM
    weights = [wm, waq, wakv, wg]
    return pl.pallas_call(
        _inproj_kernel,
        grid=(t // tm,),
        in_specs=[pl.BlockSpec((tm, d), lambda i: (i, 0))] + [_const_spec(w.shape) for w in weights],
        out_specs=[pl.BlockSpec((tm, U_WIDTH), lambda i: (i, 0)),
                   pl.BlockSpec((tm, G_WIDTH), lambda i: (i, 0))],
        out_shape=[jax.ShapeDtypeStruct((t, U_WIDTH), BF16),
                   jax.ShapeDtypeStruct((t, G_WIDTH), F32)],
        compiler_params=pltpu.CompilerParams(dimension_semantics=("arbitrary",),
                                             vmem_limit_bytes=VMEM_LIMIT),
        name="inproj",
    )(x2, *weights)


def _gateprep_kernel(g_ref, bias_ref, rows_ref, gt_ref, mp_ref):
    s = g_ref.shape[0]
    nc = s // CHUNK
    nd = 2 * M_HEADS

    def transpose_tile(i, carry):
        r0 = pl.multiple_of(i * LANES, LANES)
        tile = g_ref[pl.ds(r0, LANES), :] + bias_ref[...]
        gt_ref[:, pl.ds(r0, LANES)] = tile.T[:N_GATE, :]
        return carry

    lax.fori_loop(0, s // LANES, transpose_tile, 0)
    gt = gt_ref[...]
    ig = gt[0:nd]
    fg = gt[nd:2 * nd]
    lf = jnp.minimum(fg, 0.0) - jnp.log1p(jnp.exp(-jnp.abs(fg)))
    pos = lax.broadcasted_iota(jnp.int32, (nd, s), 1) & (CHUNK - 1)
    fwd_row = lax.broadcasted_iota(jnp.int32, (nd, s), 0) < M_HEADS

    def chunk_scan(x, op, identity):
        pre, suf = x, x
        sh = 1
        while sh < CHUNK:
            pre = op(pre, jnp.where(pos >= sh, pltpu.roll(pre, sh, 1), identity))
            suf = op(suf, jnp.where(pos < CHUNK - sh, pltpu.roll(suf, s - sh, 1), identity))
            sh *= 2
        return jnp.where(fwd_row, pre, suf)

    bcum = chunk_scan(lf, jnp.add, 0.0)
    r = ig - bcum
    rcummax = chunk_scan(r, jnp.maximum, -jnp.inf)

    def chunk_allreduce(x, op):
        sh = 1
        while sh < CHUNK:
            partner = jnp.where((pos & sh) != 0, pltpu.roll(x, sh, 1), pltpu.roll(x, s - sh, 1))
            x = op(x, partner)
            sh *= 2
        return x

    gtot = chunk_allreduce(lf, jnp.add)
    rmax = chunk_allreduce(r, jnp.maximum)
    m = jnp.zeros((M_HEADS, CHUNK), F32)
    for c in range(nc):
        sl = slice(c * CHUNK, (c + 1) * CHUNK)
        mp_ref[0:M_HEADS, sl] = m
        m = gtot[0:M_HEADS, sl] + jnp.maximum(m, rmax[0:M_HEADS, sl])
    m = jnp.zeros((M_HEADS, CHUNK), F32)
    for c in reversed(range(nc)):
        sl = slice(c * CHUNK, (c + 1) * CHUNK)
        mp_ref[M_HEADS:nd, sl] = m
        m = gtot[M_HEADS:nd, sl] + jnp.maximum(m, rmax[M_HEADS:nd, sl])
    m_in = mp_ref[...]
    mm = jnp.maximum(m_in, rmax)
    mu = jnp.maximum(rcummax, m_in)
    table = (r, mu, bcum + mu, jnp.exp(m_in - mu), jnp.exp(r - mm), jnp.exp(m_in - mm))
    pad = jnp.zeros((GATE_ROWS - 2 * len(table), s), F32)
    for h in range(M_HEADS):
        pieces = []
        for arr in table:
            pieces += [arr[h:h + 1], arr[M_HEADS + h:M_HEADS + h + 1]]
        rows_ref[h] = jnp.concatenate(pieces + [pad], axis=0)


def _gateprep(g2, bias, batch, seq):
    return pl.pallas_call(
        _gateprep_kernel,
        grid=(batch,),
        in_specs=[pl.BlockSpec((seq, G_WIDTH), lambda b: (b, 0)),
                  pl.BlockSpec((1, G_WIDTH), lambda b: (0, 0))],
        out_specs=pl.BlockSpec((None, M_HEADS, GATE_ROWS, seq), lambda b: (b, 0, 0, 0)),
        out_shape=jax.ShapeDtypeStruct((batch, M_HEADS, GATE_ROWS, seq), F32),
        scratch_shapes=[pltpu.VMEM((N_GATE, seq), F32),
                        pltpu.VMEM((2 * M_HEADS, seq), F32)],
        compiler_params=pltpu.CompilerParams(dimension_semantics=("arbitrary",),
                                             vmem_limit_bytes=VMEM_LIMIT),
        name="gateprep",
    )(g2, bias)


def _conv_silu(x_ref, w_ref, b_ref, r0, seq):
    halo = BF16_SUBLANES
    cur = x_ref[pl.ds(r0, CHUNK), :].astype(F32)
    ps = pl.multiple_of(jnp.maximum(r0 - halo, 0), halo)
    ns = pl.multiple_of(jnp.minimum(r0 + CHUNK, seq - halo), halo)
    prev = jnp.where(r0 > 0, x_ref[pl.ds(ps, halo), :].astype(F32), 0.0)
    nxt = jnp.where(r0 + CHUNK < seq, x_ref[pl.ds(ns, halo), :].astype(F32), 0.0)
    win = jnp.concatenate([prev, cur, nxt], axis=0)
    n = CHUNK + 2 * halo
    w = w_ref[...]
    y = b_ref[...] + jnp.zeros((CHUNK, LANES), F32)
    for k in range(M_CONV):
        shift = (M_CONV // 2 - k) % n
        rolled = win if shift == 0 else pltpu.roll(win, shift, 0)
        y = y + w[k:k + 1] * rolled[halo:halo + CHUNK]
    return y * _sigmoid(y)


def _mlstm_kernel(q_ref, k_ref, v_ref, o_ref, rows_ref, cwq_ref, cbq_ref, cwk_ref, cbk_ref,
                  ng_ref, out_ref, qs_ref, kt_ref, csf_ref, csb_ref, cf_ref, cb_ref):
    seq = q_ref.shape[0]
    nc = seq // CHUNK
    dh = M_HEAD_DIM

    def conv_body(i, carry):
        r0 = pl.multiple_of(i * CHUNK, CHUNK)
        qs_ref[pl.ds(r0, CHUNK), :] = _conv_silu(q_ref, cwq_ref, cbq_ref, r0, seq).astype(BF16)
        kc = _conv_silu(k_ref, cwk_ref, cbk_ref, r0, seq) * (dh ** -0.5)
        kt_ref[:, pl.ds(r0, CHUNK)] = kc.T.astype(BF16)
        return carry

    lax.fori_loop(0, nc, conv_body, 0, unroll=2)

    ones_col = (lax.broadcasted_iota(jnp.int32, (CHUNK, LANES), 1) == 0).astype(BF16)

    def v_ext(r0):
        return jnp.concatenate([v_ref[pl.ds(r0, CHUNK), :], ones_col], axis=1)

    def gate_row(kind, rev, r0):
        return rows_ref[2 * kind + rev:2 * kind + rev + 1, pl.ds(r0, CHUNK)]

    def state_step(ci, rev, c_ref, cs_ref):
        r0 = pl.multiple_of(ci * CHUNK, CHUNK)
        cs_ref[ci] = c_ref[...].astype(BF16)
        kw = (kt_ref[:, pl.ds(r0, CHUNK)].astype(F32) * gate_row(4, rev, r0)).astype(BF16)
        decay = gate_row(5, rev, r0)[:, 0:1]
        c_ref[...] = decay * c_ref[...] + jnp.dot(kw, v_ext(r0), preferred_element_type=F32)

    cf_ref[...] = jnp.zeros_like(cf_ref)
    cb_ref[...] = jnp.zeros_like(cb_ref)

    def state_body(c, carry):
        state_step(c, 0, cf_ref, csf_ref)
        state_step(nc - 1 - c, 1, cb_ref, csb_ref)
        return carry

    lax.fori_loop(0, nc, state_body, 0, unroll=2)

    t_idx = lax.broadcasted_iota(jnp.int32, (CHUNK, CHUNK), 0)
    s_idx = lax.broadcasted_iota(jnp.int32, (CHUNK, CHUNK), 1)
    tri = (s_idx <= t_idx, s_idx >= t_idx)
    eye = (s_idx == t_idx).astype(BF16)
    lane_mean = jnp.full((dh, dh), 1.0 / dh, BF16)

    def split_bf16(x, parts):
        out = []
        for _ in range(parts - 1):
            piece = x.astype(BF16)
            out.append(piece)
            x = x - piece.astype(F32)
        return out + [x.astype(BF16)]

    def mean_lanes(x):
        return sum(jnp.dot(piece, lane_mean, preferred_element_type=F32)
                   for piece in split_bf16(x, 2))

    def out_body(c, carry):
        r0 = pl.multiple_of(c * CHUNK, CHUNK)
        qc = qs_ref[pl.ds(r0, CHUNK), :]
        vext = v_ext(r0)
        qk = jnp.dot(qc, kt_ref[:, pl.ds(r0, CHUNK)], preferred_element_type=F32)
        cols = sum(lax.dot_general(eye, piece, (((1,), (1,)), ((), ())), preferred_element_type=F32)
                   for piece in split_bf16(rows_ref[:, pl.ds(r0, CHUNK)], 3))
        h = None
        for rev, cs_ref in ((0, csf_ref), (1, csb_ref)):
            mu, m_t, w_in = (cols[:, 2 * kind + rev:2 * kind + rev + 1] for kind in (1, 2, 3))
            w_intra = jnp.exp(jnp.where(tri[rev], gate_row(0, rev, r0), -jnp.inf) - mu)
            tot = jnp.dot((qk * w_intra).astype(BF16), vext, preferred_element_type=F32)
            tot = tot + w_in * jnp.dot(qc, cs_ref[c], preferred_element_type=F32)
            den = jnp.maximum(jnp.abs(tot[:, dh:dh + 1]), jnp.exp(-m_t))
            hd = tot[:, :dh] * (1.0 / den)
            h = hd if h is None else h + hd
        hc = h - mean_lanes(h)
        hn = hc * lax.rsqrt(mean_lanes(hc * hc) + MH_EPS)
        og = _sigmoid(o_ref[pl.ds(r0, CHUNK), :].astype(F32))
        out_ref[pl.ds(r0, CHUNK), :] = (hn * ng_ref[...] * og).astype(BF16)
        return carry

    lax.fori_loop(0, nc, out_body, 0, unroll=2)


def _mlstm(u, rows, conv_w, conv_b, norm_g, batch, seq):
    dh = M_HEAD_DIM
    nh = M_HEADS
    nc = seq // CHUNK

    def ucol(base):
        return pl.BlockSpec((seq, dh), lambda b, h: (b, base // dh + h))

    return pl.pallas_call(
        _mlstm_kernel,
        grid=(batch, nh),
        in_specs=[ucol(U_Q), ucol(U_K), ucol(U_V), ucol(U_O),
                  pl.BlockSpec((None, None, GATE_ROWS, seq), lambda b, h: (b, h, 0, 0)),
                  pl.BlockSpec((M_CONV, dh), lambda b, h: (0, h)),
                  pl.BlockSpec((1, dh), lambda b, h: (0, h)),
                  pl.BlockSpec((M_CONV, dh), lambda b, h: (0, nh + h)),
                  pl.BlockSpec((1, dh), lambda b, h: (0, nh + h)),
                  pl.BlockSpec((1, dh), lambda b, h: (0, h))],
        out_specs=pl.BlockSpec((seq, dh), lambda b, h: (b, h)),
        out_shape=jax.ShapeDtypeStruct((batch * seq, M_WIDTH), BF16),
        scratch_shapes=[pltpu.VMEM((seq, dh), BF16),
                        pltpu.VMEM((dh, seq), BF16),
                        pltpu.VMEM((nc, dh, dh + LANES), BF16),
                        pltpu.VMEM((nc, dh, dh + LANES), BF16),
                        pltpu.VMEM((dh, dh + LANES), F32),
                        pltpu.VMEM((dh, dh + LANES), F32)],
        compiler_params=pltpu.CompilerParams(dimension_semantics=("arbitrary", "arbitrary"),
                                             vmem_limit_bytes=VMEM_LIMIT),
        name="mlstm",
    )(u, u, u, u, rows, conv_w, conv_b, conv_w, conv_b, norm_g)


A_DELTAS = (-A_BLOCK, 0, -2 * A_BLOCK)


def _attn_kernel(sink_ref, q_ref, kv_ref, out_ref, bias_ref):
    seq = q_ref.shape[0]
    nb = seq // A_BLOCK
    span = 3 * A_BLOCK
    lo = lax.broadcasted_iota(jnp.int32, (A_BLOCK, LANES), 1) < A_HEAD_DIM
    ones_col = (lax.broadcasted_iota(jnp.int32, (span, LANES), 1) == 0).astype(BF16)
    qi =lax.broadcasted_iota(jnp.int32, (A_BLOCK, span), 0)
    ki = lax.broadcasted_iota(jnp.int32, (A_BLOCK, span), 1)
    for variant, delta in enumerate(A_DELTAS):
        rel = jnp.abs(ki - qi + delta)
        base = jnp.where(rel <= WINDOW, -rel.astype(F32), -jnp.inf)
        for head in range(A_HEADS):
            bias_ref[variant * A_HEADS + head] = base * (2.0 ** -(head + 1))

    def block(n, carry):
        q0 = pl.multiple_of(n * A_BLOCK, A_BLOCK)
        k0 = pl.multiple_of(jnp.clip(q0 - A_BLOCK, 0, seq - span), A_BLOCK)
        variant = jnp.where(n == 0, 1, jnp.where(n == nb - 1, 2, 0))
        k3 = kv_ref[pl.ds(k0, span), 0:A_KV_WIDTH]
        v3 = jnp.concatenate([kv_ref[pl.ds(k0, span), A_KV_WIDTH:2 * A_KV_WIDTH], ones_col], axis=1)
        for j in range(A_GROUP):
            qp = q_ref[pl.ds(q0, A_BLOCK), j * LANES:(j + 1) * LANES]
            halves = []
            for half in range(A_KV_HEADS):
                head = j + A_GROUP * half
                qm = jnp.where(lo if half == 0 else jnp.logical_not(lo), qp, jnp.zeros_like(qp))
                sc = lax.dot_general(qm, k3, (((1,), (1,)), ((), ())), preferred_element_type=F32)
                sc = sc + bias_ref[variant * A_HEADS + head]
                sk = sink_ref[head]
                mx = jnp.maximum(jnp.max(sc, axis=1, keepdims=True), sk)
                pv = jnp.dot(jnp.exp(sc - mx).astype(BF16), v3, preferred_element_type=F32)
                den = pv[:, LANES:LANES + 1] + jnp.exp(sk - mx)
                halves.append(pv[:, :LANES] * (1.0 / den))
            out_ref[pl.ds(q0, A_BLOCK), j * LANES:(j + 1) * LANES] = (
                jnp.where(lo, halves[0], halves[1]).astype(BF16))
        return carry

    lax.fori_loop(0, nb, block, 0)


def _attn(u, sink, batch, seq):
    return pl.pallas_call(
        _attn_kernel,
        grid=(batch,),
        in_specs=[pl.BlockSpec(memory_space=pltpu.SMEM),
                  pl.BlockSpec((seq, A_WIDTH), lambda b: (b, U_AQ // A_WIDTH)),
                  pl.BlockSpec((seq, 2 * A_KV_WIDTH), lambda b: (b, U_AKV // (2 * A_KV_WIDTH)))],
        out_specs=pl.BlockSpec((seq, A_WIDTH), lambda b: (b, 0)),
        out_shape=jax.ShapeDtypeStruct((batch * seq, A_WIDTH), BF16),
        scratch_shapes=[pltpu.VMEM((len(A_DELTAS) * A_HEADS, A_BLOCK, 3 * A_BLOCK), F32)],
        compiler_params=pltpu.CompilerParams(dimension_semantics=("arbitrary",),
                                             vmem_limit_bytes=VMEM_LIMIT),
        name="attn",
    )(sink, u, u)


def _ff_chunks(d_ff):
    step = 1024
    return [(c0, min(c0 + step, d_ff)) for c0 in range(0, d_ff, step)]


def _post_kernel(alpha, x_ref, hm_ref, ha_ref, p_ref, wom_ref, woa_ref, wg_ref, wu_ref, wd_ref,
                 wpg_ref, wpp_ref, ln1g_ref, ln1b_ref, ln2g_ref, ln2b_ref, y_ref):
    mix = jnp.dot(hm_ref[...], wom_ref[...], preferred_element_type=F32)
    mix = mix + jnp.dot(ha_ref[...], woa_ref[...], preferred_element_type=F32)
    x1 = _layer_norm(alpha * x_ref[...] + mix, ln1g_ref[...], ln1b_ref[...])
    x1b = x1.astype(BF16)
    ple_gate = _sigmoid(jnp.dot(x1b, wpg_ref[...], preferred_element_type=F32))
    acc = alpha * x1 + ple_gate * jnp.dot(p_ref[...].astype(BF16), wpp_ref[...],
                                          preferred_element_type=F32)
    for c0, c1 in _ff_chunks(wd_ref.shape[0]):
        gate = jnp.dot(x1b, wg_ref[:, c0:c1], preferred_element_type=F32)
        up = jnp.dot(x1b, wu_ref[:, c0:c1], preferred_element_type=F32)
        hid = (gate * _sigmoid(gate) * up).astype(BF16)
        acc = acc + jnp.dot(hid, wd_ref[c0:c1, :], preferred_element_type=F32)
    y_ref[...] = _layer_norm(acc, ln2g_ref[...], ln2b_ref[...])


def _post(alpha, x2, hm, ha, p2, wom, woa, wg, wu, wd, wpg, wpp, ln1g, ln1b, ln2g, ln2b):
    t, d = x2.shape
    tm = POST_TM

    def rows(width):
        return pl.BlockSpec((tm, width), lambda i: (i, 0))

    weights = [wom, woa, wg, wu, wd, wpg, wpp, ln1g, ln1b, ln2g, ln2b]
    return pl.pallas_call(
        functools.partial(_post_kernel, alpha),
        grid=(t // tm,),
        in_specs=[rows(d), rows(hm.shape[1]), rows(ha.shape[1]), rows(p2.shape[1])]
                 + [_const_spec(w.shape) for w in weights],
        out_specs=rows(d),
        out_shape=jax.ShapeDtypeStruct((t, d), F32),
        compiler_params=pltpu.CompilerParams(dimension_semantics=("arbitrary",),
                                             vmem_limit_bytes=VMEM_LIMIT),
        name="post",
    )(x2, hm, ha, p2, *weights)


def _pair_heads(w, axis):
    shape = w.shape
    w = w.reshape(shape[:axis] + (A_KV_HEADS, A_GROUP, A_HEAD_DIM) + shape[axis + 1:])
    return jnp.swapaxes(w, axis, axis + 1).reshape(shape)


def kernel(x, p, w_in, b_gate, conv_w, conv_b, mlstm_norm_g, attn_sink, w_out, ln1_g, ln1_b,
           w_ffn_in, w_ffn_out, ln2_g, ln2_b, w_ple_gate, w_ple_proj):
    batch, seq, d = x.shape
    depth = w_in.shape[0]
    d_ff = w_ffn_out.shape[1]
    t = batch * seq
    alpha = float((2 * depth) ** 0.25)
    assert seq % CHUNK == 0 and seq % A_BLOCK == 0 and seq >= 3 * A_BLOCK
    assert t % INPROJ_TM == 0 and t % POST_TM == 0
    assert w_in.shape[2] == U_WIDTH + N_GATE

    g0 = U_MAIN
    aq0 = g0 + N_GATE
    akv0 = aq0 + A_WIDTH
    wm = w_in[:, :, :g0].astype(BF16)
    waq = (_pair_heads(w_in[:, :, aq0:akv0], 2) * (A_HEAD_DIM ** -0.5)).astype(BF16)
    wakv = w_in[:, :, akv0:].astype(BF16)
    wgate = jnp.pad(w_in[:, :, g0:aq0], ((0, 0), (0, 0), (0, G_WIDTH - N_GATE))).astype(BF16)
    gate_bias = jnp.pad(b_gate, ((0, 0), (0, G_WIDTH - N_GATE)))[:, None, :]
    wom = w_out[:, :M_WIDTH].astype(BF16)
    woa = _pair_heads(w_out[:, M_WIDTH:], 1).astype(BF16)
    wg = w_ffn_in[:, :, :d_ff].astype(BF16)
    wu = w_ffn_in[:, :, d_ff:].astype(BF16)
    wd = w_ffn_out.astype(BF16)
    wpg = w_ple_gate.astype(BF16)
    wpp = w_ple_proj.astype(BF16)

    h = x.reshape(t, d)
    p2 = p.reshape(depth, t, p.shape[-1])
    for i in range(depth):
        u, g = _inproj(h, wm[i], waq[i], wakv[i], wgate[i])
        rows = _gateprep(g, gate_bias[i], batch, seq)
        hm = _mlstm(u, rows, conv_w[i], conv_b[i][None], mlstm_norm_g[i][None], batch, seq)
        ha = _attn(u, attn_sink[i], batch, seq)
        h = _post(alpha, h, hm, ha, p2[i], wom[i], woa[i], wg[i], wu[i], wd[i], wpg[i], wpp[i],
                  ln1_g[i][None], ln1_b[i][None], ln2_g[i][None], ln2_b[i][None])
    return h.reshape(batch, seq, d)
```

```python
import functools

import jax
import jax.numpy as jnp
from jax import lax
from jax.experimental import pallas as pl
from jax.experimental.pallas import tpu as pltpu

F32 = jnp.float32
BF16 = jnp.bfloat16

M_HEADS = 4
M_HEAD_DIM = 128
M_WIDTH = M_HEADS * M_HEAD_DIM
M_CONV = 5
N_GATE = 4 * M_HEADS
A_HEADS = 8
A_KV_HEADS = 2
A_HEAD_DIM = 64
A_GROUP = A_HEADS // A_KV_HEADS
A_WIDTH = A_HEADS * A_HEAD_DIM
A_KV_WIDTH = A_KV_HEADS * A_HEAD_DIM
WINDOW = 128
LN_EPS = 1e-5
MH_EPS = 1e-6

LANES = 128
BF16_SUBLANES = 16

CHUNK = 256
A_BLOCK = 128
U_MAIN = 4 * M_WIDTH
U_WIDTH = U_MAIN + A_WIDTH + 2 * A_KV_WIDTH
G_WIDTH = LANES
U_Q, U_K, U_V, U_O = 0, M_WIDTH, 2 * M_WIDTH, 3 * M_WIDTH
U_AQ = U_MAIN
U_AKV = U_AQ + A_WIDTH
GATE_ROWS = 16
M_EXT = BF16_SUBLANES
INPROJ_TM = 1024
POST_TM = 512
VMEM_LIMIT = 56 * 1024 * 1024


def _sigmoid(x):
    return 1.0 / (1.0 + jnp.exp(-x))


def _layer_norm(z, g, b):
    mu = jnp.mean(z, axis=-1, keepdims=True)
    zc = z - mu
    var = jnp.mean(zc * zc, axis=-1, keepdims=True)
    return zc * lax.rsqrt(var + LN_EPS) * g + b


def _layer_spec(stacked, layer):
    _, rows, cols = stacked.shape
    return pl.BlockSpec((None, rows, cols), lambda *_: (layer, 0, 0), pipeline_mode=pl.Buffered(1))


def _inproj_kernel(x_ref, wm_ref, waq_ref, wakv_ref, wg_ref, u_ref, g_ref):
    xb = x_ref[...].astype(BF16)
    for c0 in range(0, U_MAIN, 512):
        u_ref[:, c0:c0 + 512] = jnp.dot(
            xb, wm_ref[:, c0:c0 + 512], preferred_element_type=F32).astype(BF16)
    u_ref[:, U_AQ:U_AKV] = jnp.dot(xb, waq_ref[...], preferred_element_type=F32).astype(BF16)
    u_ref[:, U_AKV:] = jnp.dot(xb, wakv_ref[...], preferred_element_type=F32).astype(BF16)
    g_ref[...] = jnp.dot(xb, wg_ref[...], preferred_element_type=F32)


def _inproj(layer, x2, wm, waq, wakv, wg):
    t, d = x2.shape
    tm = INPROJ_TM
    weights = [wm, waq, wakv, wg]
    return pl.pallas_call(
        _inproj_kernel,
        grid=(t // tm,),
        in_specs=[pl.BlockSpec((tm, d), lambda i: (i, 0))] + [_layer_spec(w, layer) for w in weights],
        out_specs=[pl.BlockSpec((tm, U_WIDTH), lambda i: (i, 0)),
                   pl.BlockSpec((tm, G_WIDTH), lambda i: (i, 0))],
        out_shape=[jax.ShapeDtypeStruct((t, U_WIDTH), BF16),
                   jax.ShapeDtypeStruct((t, G_WIDTH), F32)],
        compiler_params=pltpu.CompilerParams(dimension_semantics=("arbitrary",),
                                             vmem_limit_bytes=VMEM_LIMIT),
        name="inproj",
    )(x2, *weights)


def _gateprep_kernel(g_ref, bias_ref, rows_ref, gt_ref, mp_ref):
    s = g_ref.shape[0]
    nc = s // CHUNK
    nd = 2 * M_HEADS

    def transpose_tile(i, carry):
        r0 = pl.multiple_of(i * LANES, LANES)
        tile = g_ref[pl.ds(r0, LANES), :] + bias_ref[...]
        gt_ref[:, pl.ds(r0, LANES)] = tile.T[:N_GATE, :]
        return carry

    lax.fori_loop(0, s // LANES, transpose_tile, 0)
    gt = gt_ref[...]
    ig = gt[0:nd]
    fg = gt[nd:2 * nd]
    lf = jnp.minimum(fg, 0.0) - jnp.log1p(jnp.exp(-jnp.abs(fg)))
    pos = lax.broadcasted_iota(jnp.int32, (nd, s), 1) & (CHUNK - 1)
    fwd_row = lax.broadcasted_iota(jnp.int32, (nd, s), 0) < M_HEADS

    def chunk_scan(x, op, identity):
        pre, suf = x, x
        sh = 1
        while sh < CHUNK:
            pre = op(pre, jnp.where(pos >= sh, pltpu.roll(pre, sh, 1), identity))
            suf = op(suf, jnp.where(pos < CHUNK - sh, pltpu.roll(suf, s - sh, 1), identity))
            sh *= 2
        return jnp.where(fwd_row, pre, suf)

    bcum = chunk_scan(lf, jnp.add, 0.0)
    r = ig - bcum
    rcummax = chunk_scan(r, jnp.maximum, -jnp.inf)

    def chunk_allreduce(x, op):
        sh = 1
        while sh < CHUNK:
            partner = jnp.where((pos & sh) != 0, pltpu.roll(x, sh, 1), pltpu.roll(x, s - sh, 1))
            x = op(x, partner)
            sh *= 2
        return x

    gtot = chunk_allreduce(lf, jnp.add)
    rmax = chunk_allreduce(r, jnp.maximum)
    m = jnp.zeros((M_HEADS, CHUNK), F32)
    for c in range(nc):
        sl = slice(c * CHUNK, (c + 1) * CHUNK)
        mp_ref[0:M_HEADS, sl] = m
        m = gtot[0:M_HEADS, sl] + jnp.maximum(m, rmax[0:M_HEADS, sl])
    m = jnp.zeros((M_HEADS, CHUNK), F32)
    for c in reversed(range(nc)):
        sl = slice(c * CHUNK, (c + 1) * CHUNK)
        mp_ref[M_HEADS:nd, sl] = m
        m = gtot[M_HEADS:nd, sl] + jnp.maximum(m, rmax[M_HEADS:nd, sl])
    m_in = mp_ref[...]
    mm = jnp.maximum(m_in, rmax)
    mu = jnp.maximum(rcummax, m_in)
    table = (r, mu, bcum + mu, jnp.exp(m_in - mu), jnp.exp(r - mm), jnp.exp(m_in - mm))
    pad = jnp.zeros((GATE_ROWS - 2 * len(table), s), F32)
    for h in range(M_HEADS):
        pieces = []
        for arr in table:
            pieces += [arr[h:h + 1], arr[M_HEADS + h:M_HEADS + h + 1]]
        rows_ref[h] = jnp.concatenate(pieces + [pad], axis=0)


def _gateprep(layer, g2, bias, batch, seq):
    return pl.pallas_call(
        _gateprep_kernel,
        grid=(batch,),
        in_specs=[pl.BlockSpec((seq, G_WIDTH), lambda b: (b, 0)),
                  pl.BlockSpec((None, 1, G_WIDTH), lambda b: (layer, 0, 0))],
        out_specs=pl.BlockSpec((None, M_HEADS, GATE_ROWS, seq), lambda b: (b, 0, 0, 0)),
        out_shape=jax.ShapeDtypeStruct((batch, M_HEADS, GATE_ROWS, seq), F32),
        scratch_shapes=[pltpu.VMEM((N_GATE, seq), F32),
                        pltpu.VMEM((2 * M_HEADS, seq), F32)],
        compiler_params=pltpu.CompilerParams(dimension_semantics=("arbitrary",),
                                             vmem_limit_bytes=VMEM_LIMIT),
        name="gateprep",
    )(g2, bias)


def _conv_silu(x_ref, w_ref, b_ref, r0, seq):
    halo = BF16_SUBLANES
    cur = x_ref[pl.ds(r0, CHUNK), :].astype(F32)
    ps = pl.multiple_of(jnp.maximum(r0 - halo, 0), halo)
    ns = pl.multiple_of(jnp.minimum(r0 + CHUNK, seq - halo), halo)
    prev = jnp.where(r0 > 0, x_ref[pl.ds(ps, halo), :].astype(F32), 0.0)
    nxt = jnp.where(r0 + CHUNK < seq, x_ref[pl.ds(ns, halo), :].astype(F32), 0.0)
    win = jnp.concatenate([prev, cur, nxt], axis=0)
    n = CHUNK + 2 * halo
    w = w_ref[...]
    y = b_ref[...] + jnp.zeros((CHUNK, LANES), F32)
    for k in range(M_CONV):
        shift = (M_CONV // 2 - k) % n
        rolled = win if shift == 0 else pltpu.roll(win, shift, 0)
        y = y + w[k:k + 1] * rolled[halo:halo + CHUNK]
    return y * _sigmoid(y)


def _mlstm_kernel(q_ref, k_ref, v_ref, o_ref, rows_ref, cwq_ref, cbq_ref, cwk_ref, cbk_ref,
                  ng_ref, out_ref, qt_ref, ks_ref, vt_ref, csf_ref, csb_ref, cf_ref, cb_ref):
    seq = q_ref.shape[0]
    nc = seq // CHUNK
    dh = M_HEAD_DIM
    de = dh + M_EXT

    vt_ref[dh:de, :] = (lax.broadcasted_iota(jnp.int32, (M_EXT, seq), 0) == 0).astype(BF16)

    def conv_body(i, carry):
        r0 = pl.multiple_of(i * CHUNK, CHUNK)
        rows = pl.ds(r0, CHUNK)
        qt_ref[:, rows] = _conv_silu(q_ref, cwq_ref, cbq_ref, r0, seq).T.astype(BF16)
        ks_ref[rows, :] = (_conv_silu(k_ref, cwk_ref, cbk_ref, r0, seq) * (dh ** -0.5)).astype(BF16)
        vt_ref[0:dh, rows] = v_ref[rows, :].astype(F32).T.astype(BF16)
        return carry

    lax.fori_loop(0, nc, conv_body, 0, unroll=2)

    def gate_row(kind, rev, r0):
        return rows_ref[2 * kind + rev:2 * kind + rev + 1, pl.ds(r0, CHUNK)]

    def state_step(ci, rev, c_ref, cs_ref):
        r0 = pl.multiple_of(ci * CHUNK, CHUNK)
        cs_ref[ci] = c_ref[...].astype(BF16)
        vw = (vt_ref[:, pl.ds(r0, CHUNK)].astype(F32) * gate_row(4, rev, r0)).astype(BF16)
        decay = gate_row(5, rev, r0)[:, 0:1]
        c_ref[...] = decay * c_ref[...] + jnp.dot(vw, ks_ref[pl.ds(r0, CHUNK), :],
                                                  preferred_element_type=F32)

    cf_ref[...] = jnp.zeros_like(cf_ref)
    cb_ref[...] = jnp.zeros_like(cb_ref)

    def state_body(c, carry):
        state_step(c, 0, cf_ref, csf_ref)
        state_step(nc - 1 - c, 1, cb_ref, csb_ref)
        return carry

    lax.fori_loop(0, nc, state_body, 0, unroll=4)

    s_idx = lax.broadcasted_iota(jnp.int32, (CHUNK, CHUNK), 0)
    t_idx = lax.broadcasted_iota(jnp.int32, (CHUNK, CHUNK), 1)
    tri = (s_idx <= t_idx, s_idx >= t_idx)
    krow = lax.broadcasted_iota(jnp.int32, (BF16_SUBLANES, CHUNK), 0)

    def split_rows(x, first):
        hi = x.astype(BF16).astype(F32)
        mid = (x - hi).astype(BF16).astype(F32)
        lo = x - hi - mid
        out = jnp.where(krow == first, hi, jnp.where(krow == first + 1, mid,
                                                     jnp.where(krow == first + 2, lo, 0.0)))
        ones = (krow >= 3 - first) & (krow < 6 - first)
        return jnp.where(ones, 1.0, out).astype(BF16)

    def out_body(c, carry):
        r0 = pl.multiple_of(c * CHUNK, CHUNK)
        rows = pl.ds(r0, CHUNK)
        qt = qt_ref[:, rows]
        vxt = vt_ref[:, rows]
        qk = jnp.dot(ks_ref[rows, :], qt, preferred_element_type=F32)
        h = None
        for rev, cs_ref in ((0, csf_ref), (1, csb_ref)):
            a = lax.dot_general(split_rows(gate_row(0, rev, r0), 0),
                                split_rows(-gate_row(1, rev, r0), 3),
                                (((0,), (0,)), ((), ())), preferred_element_type=F32)
            w_intra = jnp.exp(jnp.where(tri[rev], a, -jnp.inf))
            tot = jnp.dot(vxt, (qk * w_intra).astype(BF16), preferred_element_type=F32)
            tot = tot + gate_row(3, rev, r0) * jnp.dot(cs_ref[c], qt, preferred_element_type=F32)
            den = jnp.maximum(jnp.abs(tot[dh:dh + 1, :]), jnp.exp(-gate_row(2, rev, r0)))
            hd = tot[:dh, :] * (1.0 / den)
            h = hd if h is None else h + hd
        hc = h - jnp.mean(h, axis=0, keepdims=True)
        hn = hc * lax.rsqrt(jnp.mean(hc * hc, axis=0, keepdims=True) + MH_EPS)
        og = _sigmoid(o_ref[rows, :].astype(F32))
        out_ref[rows, :] = (hn.T * ng_ref[...] * og).astype(BF16)
        return carry

    lax.fori_loop(0, nc, out_body, 0, unroll=4)


def _mlstm(layer, u, rows, conv_w, conv_b, norm_g, batch, seq):
    dh = M_HEAD_DIM
    de = dh + M_EXT
    nh = M_HEADS
    nc = seq // CHUNK

    def ucol(base):
        return pl.BlockSpec((seq, dh), lambda b, h: (b, base // dh + h))

    return pl.pallas_call(
        _mlstm_kernel,
        grid=(batch, nh),
        in_specs=[ucol(U_Q), ucol(U_K), ucol(U_V), ucol(U_O),
                  pl.BlockSpec((None, None, GATE_ROWS, seq), lambda b, h: (b, h, 0, 0)),
                  pl.BlockSpec((None, M_CONV, dh), lambda b, h: (layer, 0, h)),
                  pl.BlockSpec((None, 1, dh), lambda b, h: (layer, 0, h)),
                  pl.BlockSpec((None, M_CONV, dh), lambda b, h: (layer, 0, nh + h)),
                  pl.BlockSpec((None, 1, dh), lambda b, h: (layer, 0, nh + h)),
                  pl.BlockSpec((None, 1, dh), lambda b, h: (layer, 0, h))],
        out_specs=pl.BlockSpec((seq, dh), lambda b, h: (b, h)),
        out_shape=jax.ShapeDtypeStruct((batch * seq, M_WIDTH), BF16),
        scratch_shapes=[pltpu.VMEM((dh, seq), BF16),
                        pltpu.VMEM((seq, dh), BF16),
                        pltpu.VMEM((de, seq), BF16),
                        pltpu.VMEM((nc, de, dh), BF16),
                        pltpu.VMEM((nc, de, dh), BF16),
                        pltpu.VMEM((de, dh), F32),
                        pltpu.VMEM((de, dh), F32)],
        compiler_params=pltpu.CompilerParams(dimension_semantics=("arbitrary", "arbitrary"),
                                             vmem_limit_bytes=VMEM_LIMIT),
        name="mlstm",
    )(u, u, u, u, rows, conv_w, conv_b, conv_w, conv_b, norm_g)


A_DELTAS = (-A_BLOCK, 0, -2 * A_BLOCK)


def _attn_kernel(layer, sink_ref, q_ref, kv_ref, out_ref, bias_ref):
    seq = q_ref.shape[0]
    nb = seq // A_BLOCK
    span = 3 * A_BLOCK
    lo = lax.broadcasted_iota(jnp.int32, (A_BLOCK, LANES), 1) < A_HEAD_DIM
    ones_col = (lax.broadcasted_iota(jnp.int32, (span, LANES), 1) == 0).astype(BF16)
    qi =lax.broadcasted_iota(jnp.int32, (A_BLOCK, span), 0)
    ki = lax.broadcasted_iota(jnp.int32, (A_BLOCK, span), 1)
    for variant, delta in enumerate(A_DELTAS):
        rel = jnp.abs(ki - qi + delta)
        base = jnp.where(rel <= WINDOW, -rel.astype(F32), -jnp.inf)
        for head in range(A_HEADS):
            bias_ref[variant * A_HEADS + head] = base * (2.0 ** -(head + 1))

    def block(n, carry):
        q0 = pl.multiple_of(n * A_BLOCK, A_BLOCK)
        k0 = pl.multiple_of(jnp.clip(q0 - A_BLOCK, 0, seq - span), A_BLOCK)
        variant = jnp.where(n == 0, 1, jnp.where(n == nb - 1, 2, 0))
        k3 = kv_ref[pl.ds(k0, span), 0:A_KV_WIDTH]
        v3 = jnp.concatenate([kv_ref[pl.ds(k0, span), A_KV_WIDTH:2 * A_KV_WIDTH], ones_col], axis=1)
        for j in range(A_GROUP):
            qp = q_ref[pl.ds(q0, A_BLOCK), j * LANES:(j + 1) * LANES]
            halves = []
            for half in range(A_KV_HEADS):
                head = j + A_GROUP * half
                qm = jnp.where(lo if half == 0 else jnp.logical_not(lo), qp, jnp.zeros_like(qp))
                sc = lax.dot_general(qm, k3, (((1,), (1,)), ((), ())), preferred_element_type=F32)
                sc = sc + bias_ref[variant * A_HEADS + head]
                sk = sink_ref[layer, head]
                mx = jnp.maximum(jnp.max(sc, axis=1, keepdims=True), sk)
                pv = jnp.dot(jnp.exp(sc - mx).astype(BF16), v3, preferred_element_type=F32)
                den = pv[:, LANES:LANES + 1] + jnp.exp(sk - mx)
                halves.append(pv[:, :LANES] * (1.0 / den))
            out_ref[pl.ds(q0, A_BLOCK), j * LANES:(j + 1) * LANES] = (
                jnp.where(lo, halves[0], halves[1]).astype(BF16))
        return carry

    lax.fori_loop(0, nb, block, 0)


def _attn(layer, u, sink, batch, seq):
    return pl.pallas_call(
        functools.partial(_attn_kernel, layer),
        grid=(batch,),
        in_specs=[pl.BlockSpec(memory_space=pltpu.SMEM),
                  pl.BlockSpec((seq, A_WIDTH), lambda b: (b, U_AQ // A_WIDTH)),
                  pl.BlockSpec((seq, 2 * A_KV_WIDTH), lambda b: (b, U_AKV // (2 * A_KV_WIDTH)))],
        out_specs=pl.BlockSpec((seq, A_WIDTH), lambda b: (b, 0)),
        out_shape=jax.ShapeDtypeStruct((batch * seq, A_WIDTH), BF16),
        scratch_shapes=[pltpu.VMEM((len(A_DELTAS) * A_HEADS, A_BLOCK, 3 * A_BLOCK), F32)],
        compiler_params=pltpu.CompilerParams(dimension_semantics=("arbitrary",),
                                             vmem_limit_bytes=VMEM_LIMIT),
        name="attn",
    )(sink, u, u)


def _ff_chunks(d_ff):
    step = 1024
    return [(c0, min(c0 + step, d_ff)) for c0 in range(0, d_ff, step)]


def _post_kernel(alpha, x_ref, hm_ref, ha_ref, p_ref, wom_ref, woa_ref, wg_ref, wu_ref, wd_ref,
                 wpg_ref, wpp_ref, ln1g_ref, ln1b_ref, ln2g_ref, ln2b_ref, y_ref):
    mix = jnp.dot(hm_ref[...], wom_ref[...], preferred_element_type=F32)
    mix = mix + jnp.dot(ha_ref[...], woa_ref[...], preferred_element_type=F32)
    x1 = _layer_norm(alpha * x_ref[...] + mix, ln1g_ref[...], ln1b_ref[...])
    x1b = x1.astype(BF16)
    ple_gate = _sigmoid(jnp.dot(x1b, wpg_ref[...], preferred_element_type=F32))
    acc = alpha * x1 + ple_gate * jnp.dot(p_ref[...].astype(BF16), wpp_ref[...],
                                          preferred_element_type=F32)
    for c0, c1 in _ff_chunks(wd_ref.shape[0]):
        gate = jnp.dot(x1b, wg_ref[:, c0:c1], preferred_element_type=F32)
        up = jnp.dot(x1b, wu_ref[:, c0:c1], preferred_element_type=F32)
        hid = (gate * _sigmoid(gate) * up).astype(BF16)
        acc = acc + jnp.dot(hid, wd_ref[c0:c1, :], preferred_element_type=F32)
    y_ref[...] = _layer_norm(acc, ln2g_ref[...], ln2b_ref[...])


def _post(layer, alpha, x2, hm, ha, p3, wom, woa, wg, wu, wd, wpg, wpp, ln1g, ln1b, ln2g, ln2b):
    t, d = x2.shape
    tm = POST_TM

    def rows(width):
        return pl.BlockSpec((tm, width), lambda i: (i, 0))

    weights = [wom, woa, wg, wu, wd, wpg, wpp, ln1g, ln1b, ln2g, ln2b]
    return pl.pallas_call(
        functools.partial(_post_kernel, alpha),
        grid=(t // tm,),
        in_specs=[rows(d), rows(hm.shape[1]), rows(ha.shape[1]),
                  pl.BlockSpec((None, tm, p3.shape[2]), lambda i: (layer, i, 0))]
                 + [_layer_spec(w, layer) for w in weights],
        out_specs=rows(d),
        out_shape=jax.ShapeDtypeStruct((t, d), F32),
        compiler_params=pltpu.CompilerParams(dimension_semantics=("arbitrary",),
                                             vmem_limit_bytes=VMEM_LIMIT),
        name="post",
    )(x2, hm, ha, p3, *weights)


def _pair_heads(w, axis):
    shape = w.shape
    w = w.reshape(shape[:axis] + (A_KV_HEADS, A_GROUP, A_HEAD_DIM) + shape[axis + 1:])
    return jnp.swapaxes(w, axis, axis + 1).reshape(shape)


def kernel(x, p, w_in, b_gate, conv_w, conv_b, mlstm_norm_g, attn_sink, w_out, ln1_g, ln1_b,
           w_ffn_in, w_ffn_out, ln2_g, ln2_b, w_ple_gate, w_ple_proj):
    batch, seq, d = x.shape
    depth = w_in.shape[0]
    d_ff = w_ffn_out.shape[1]
    t = batch * seq
    alpha = float((2 * depth) ** 0.25)
    assert seq % CHUNK == 0 and seq % A_BLOCK == 0 and seq >= 3 * A_BLOCK
    assert t % INPROJ_TM == 0 and t % POST_TM == 0
    assert w_in.shape[2] == U_WIDTH + N_GATE

    g0 = U_MAIN
    aq0 = g0 + N_GATE
    akv0 = aq0 + A_WIDTH
    wm = w_in[:, :, :g0].astype(BF16)
    waq = (_pair_heads(w_in[:, :, aq0:akv0], 2) * (A_HEAD_DIM ** -0.5)).astype(BF16)
    wakv = w_in[:, :, akv0:].astype(BF16)
    wgate = jnp.pad(w_in[:, :, g0:aq0], ((0, 0), (0, 0), (0, G_WIDTH - N_GATE))).astype(BF16)
    gate_bias = jnp.pad(b_gate, ((0, 0), (0, G_WIDTH - N_GATE)))[:, None, :]
    wom = w_out[:, :M_WIDTH].astype(BF16)
    woa = _pair_heads(w_out[:, M_WIDTH:], 1).astype(BF16)
    wg = w_ffn_in[:, :, :d_ff].astype(BF16)
    wu = w_ffn_in[:, :, d_ff:].astype(BF16)
    wd = w_ffn_out.astype(BF16)
    wpg = w_ple_gate.astype(BF16)
    wpp = w_ple_proj.astype(BF16)
    conv_b3, norm_g3, ln1_g3, ln1_b3, ln2_g3, ln2_b3 = (
        a[:, None, :] for a in (conv_b, mlstm_norm_g, ln1_g, ln1_b, ln2_g, ln2_b))

    h = x.reshape(t, d)
    p3 = p.reshape(depth, t, p.shape[-1])
    for i in range(depth):
        u, g = _inproj(i, h, wm, waq, wakv, wgate)
        rows = _gateprep(i, g, gate_bias, batch, seq)
        hm = _mlstm(i, u, rows, conv_w, conv_b3, norm_g3, batch, seq)
        ha = _attn(i, u, attn_sink, batch, seq)
        h = _post(i, alpha, h, hm, ha, p3, wom, woa, wg, wu, wd, wpg, wpp,
                  ln1_g3, ln1_b3, ln2_g3, ln2_b3)
    return h.reshape(batch, seq, d)
```

```python
import functools

import jax
import jax.numpy as jnp
from jax import lax
from jax.experimental import pallas as pl
from jax.experimental.pallas import tpu as pltpu

F32 = jnp.float32
BF16 = jnp.bfloat16

M_HEADS = 4
M_HEAD_DIM = 128
M_WIDTH = M_HEADS * M_HEAD_DIM
M_CONV = 5
N_GATE = 4 * M_HEADS
A_HEADS = 8
A_KV_HEADS = 2
A_HEAD_DIM = 64
A_GROUP = A_HEADS // A_KV_HEADS
A_WIDTH = A_HEADS * A_HEAD_DIM
A_KV_WIDTH = A_KV_HEADS * A_HEAD_DIM
WINDOW = 128
LN_EPS = 1e-5
MH_EPS = 1e-6

LANES = 128
BF16_SUBLANES = 16

CHUNK = 256
A_BLOCK = 128
U_MAIN = 4 * M_WIDTH
U_WIDTH = U_MAIN + A_WIDTH + 2 * A_KV_WIDTH
G_WIDTH = LANES
U_Q, U_K, U_V, U_O = 0, M_WIDTH, 2 * M_WIDTH, 3 * M_WIDTH
U_AQ = U_MAIN
U_AKV = U_AQ + A_WIDTH
GATE_ROWS = 16
M_OUT_GROUP = 4
M_EXT = BF16_SUBLANES
INPROJ_TM = 1024
POST_TM = 512
POST_SUB = 256
VMEM_LIMIT = 56 * 1024 * 1024


def _sigmoid(x):
    return 1.0 / (1.0 + jnp.exp(-x))


def _layer_norm(z, g, b):
    mu = jnp.mean(z, axis=-1, keepdims=True)
    zc = z - mu
    var = jnp.mean(zc * zc, axis=-1, keepdims=True)
    return zc * lax.rsqrt(var + LN_EPS) * g + b


def _layer_spec(stacked, layer):
    _, rows, cols = stacked.shape
    return pl.BlockSpec((None, rows, cols), lambda *_: (layer, 0, 0), pipeline_mode=pl.Buffered(1))


def _inproj_kernel(x_ref, wm_ref, waq_ref, wakv_ref, wg_ref, u_ref, g_ref):
    xb = x_ref[...].astype(BF16)
    for c0 in range(0, U_MAIN, 512):
        u_ref[:, c0:c0 + 512] = jnp.dot(
            xb, wm_ref[:, c0:c0 + 512], preferred_element_type=F32).astype(BF16)
    u_ref[:, U_AQ:U_AKV] = jnp.dot(xb, waq_ref[...], preferred_element_type=F32).astype(BF16)
    u_ref[:, U_AKV:] = jnp.dot(xb, wakv_ref[...], preferred_element_type=F32).astype(BF16)
    g_ref[...] = jnp.dot(xb, wg_ref[...], preferred_element_type=F32)


def _inproj(layer, x2, wm, waq, wakv, wg):
    t, d = x2.shape
    tm = INPROJ_TM
    weights = [wm, waq, wakv, wg]
    return pl.pallas_call(
        _inproj_kernel,
        grid=(t // tm,),
        in_specs=[pl.BlockSpec((tm, d), lambda i: (i, 0))] + [_layer_spec(w, layer) for w in weights],
        out_specs=[pl.BlockSpec((tm, U_WIDTH), lambda i: (i, 0)),
                   pl.BlockSpec((tm, G_WIDTH), lambda i: (i, 0))],
        out_shape=[jax.ShapeDtypeStruct((t, U_WIDTH), BF16),
                   jax.ShapeDtypeStruct((t, G_WIDTH), F32)],
        compiler_params=pltpu.CompilerParams(dimension_semantics=("arbitrary",),
                                             vmem_limit_bytes=VMEM_LIMIT),
        name="inproj",
    )(x2, *weights)


def _gateprep_kernel(g_ref, bias_ref, rows_ref, gt_ref, mp_ref):
    s = g_ref.shape[0]
    nc = s // CHUNK
    nd = 2 * M_HEADS

    def transpose_tile(i, carry):
        r0 = pl.multiple_of(i * LANES, LANES)
        tile = g_ref[pl.ds(r0, LANES), :] + bias_ref[...]
        gt_ref[:, pl.ds(r0, LANES)] = tile.T[:N_GATE, :]
        return carry

    lax.fori_loop(0, s // LANES, transpose_tile, 0)
    gt = gt_ref[...]
    ig = gt[0:nd]
    fg = gt[nd:2 * nd]
    lf = jnp.minimum(fg, 0.0) - jnp.log1p(jnp.exp(-jnp.abs(fg)))
    pos = lax.broadcasted_iota(jnp.int32, (nd, s), 1) & (CHUNK - 1)
    fwd_row = lax.broadcasted_iota(jnp.int32, (nd, s), 0) < M_HEADS

    def chunk_scan(x, op, identity):
        pre, suf = x, x
        sh = 1
        while sh < CHUNK:
            pre = op(pre, jnp.where(pos >= sh, pltpu.roll(pre, sh, 1), identity))
            suf = op(suf, jnp.where(pos < CHUNK - sh, pltpu.roll(suf, s - sh, 1), identity))
            sh *= 2
        return jnp.where(fwd_row, pre, suf)

    bcum = chunk_scan(lf, jnp.add, 0.0)
    r = ig - bcum
    rcummax = chunk_scan(r, jnp.maximum, -jnp.inf)

    def chunk_allreduce(x, op):
        sh = 1
        while sh < CHUNK:
            partner = jnp.where((pos & sh) != 0, pltpu.roll(x, sh, 1), pltpu.roll(x, s - sh, 1))
            x = op(x, partner)
            sh *= 2
        return x

    gtot = chunk_allreduce(lf, jnp.add)
    rmax = chunk_allreduce(r, jnp.maximum)
    m = jnp.zeros((M_HEADS, CHUNK), F32)
    for c in range(nc):
        sl = slice(c * CHUNK, (c + 1) * CHUNK)
        mp_ref[0:M_HEADS, sl] = m
        m = gtot[0:M_HEADS, sl] + jnp.maximum(m, rmax[0:M_HEADS, sl])
    m = jnp.zeros((M_HEADS, CHUNK), F32)
    for c in reversed(range(nc)):
        sl = slice(c * CHUNK, (c + 1) * CHUNK)
        mp_ref[M_HEADS:nd, sl] = m
        m = gtot[M_HEADS:nd, sl] + jnp.maximum(m, rmax[M_HEADS:nd, sl])
    m_in = mp_ref[...]
    mm = jnp.maximum(m_in, rmax)
    mu = jnp.maximum(rcummax, m_in)
    table = (r, mu, bcum + mu, jnp.exp(m_in - mu), jnp.exp(r - mm), jnp.exp(m_in - mm))
    pad = jnp.zeros((GATE_ROWS - 2 * len(table), s), F32)
    for h in range(M_HEADS):
        pieces = []
        for arr in table:
            pieces += [arr[h:h + 1], arr[M_HEADS + h:M_HEADS + h + 1]]
        rows_ref[h] = jnp.concatenate(pieces + [pad], axis=0)


def _gateprep(layer, g2, bias, batch, seq):
    return pl.pallas_call(
        _gateprep_kernel,
        grid=(batch,),
        in_specs=[pl.BlockSpec((seq, G_WIDTH), lambda b: (b, 0)),
                  pl.BlockSpec((None, 1, G_WIDTH), lambda b: (layer, 0, 0))],
        out_specs=pl.BlockSpec((None, M_HEADS, GATE_ROWS, seq), lambda b: (b, 0, 0, 0)),
        out_shape=jax.ShapeDtypeStruct((batch, M_HEADS, GATE_ROWS, seq), F32),
        scratch_shapes=[pltpu.VMEM((N_GATE, seq), F32),
                        pltpu.VMEM((2 * M_HEADS, seq), F32)],
        compiler_params=pltpu.CompilerParams(dimension_semantics=("arbitrary",),
                                             vmem_limit_bytes=VMEM_LIMIT),
        name="gateprep",
    )(g2, bias)


def _conv_silu(x_ref, w_ref, b_ref, r0, seq):
    halo = BF16_SUBLANES
    cur = x_ref[pl.ds(r0, CHUNK), :].astype(F32)
    ps = pl.multiple_of(jnp.maximum(r0 - halo, 0), halo)
    ns = pl.multiple_of(jnp.minimum(r0 + CHUNK, seq - halo), halo)
    prev = jnp.where(r0 > 0, x_ref[pl.ds(ps, halo), :].astype(F32), 0.0)
    nxt = jnp.where(r0 + CHUNK < seq, x_ref[pl.ds(ns, halo), :].astype(F32), 0.0)
    win = jnp.concatenate([prev, cur, nxt], axis=0)
    n = CHUNK + 2 * halo
    w = w_ref[...]
    y = b_ref[...] + jnp.zeros((CHUNK, LANES), F32)
    for k in range(M_CONV):
        shift = (M_CONV // 2 - k) % n
        rolled = win if shift == 0 else pltpu.roll(win, shift, 0)
        y = y + w[k:k + 1] * rolled[halo:halo + CHUNK]
    return y * _sigmoid(y)


def _mlstm_kernel(q_ref, k_ref, v_ref, o_ref, rows_ref, cwq_ref, cbq_ref, cwk_ref, cbk_ref,
                  ng_ref, out_ref, qt_ref, ks_ref, vt_ref, csf_ref, csb_ref, cf_ref, cb_ref):
    seq = q_ref.shape[0]
    nc = seq // CHUNK
    dh = M_HEAD_DIM
    de = dh + M_EXT

    vt_ref[dh:de, :] = (lax.broadcasted_iota(jnp.int32, (M_EXT, seq), 0) == 0).astype(BF16)

    def conv_body(i, carry):
        r0 = pl.multiple_of(i * CHUNK, CHUNK)
        rows = pl.ds(r0, CHUNK)
        qt_ref[:, rows] = _conv_silu(q_ref, cwq_ref, cbq_ref, r0, seq).T.astype(BF16)
        ks_ref[rows, :] = (_conv_silu(k_ref, cwk_ref, cbk_ref, r0, seq) * (dh ** -0.5)).astype(BF16)
        vt_ref[0:dh, rows] = v_ref[rows, :].astype(F32).T.astype(BF16)
        return carry

    lax.fori_loop(0, nc, conv_body, 0, unroll=2)

    def gate_row(kind, rev, r0):
        return rows_ref[2 * kind + rev:2 * kind + rev + 1, pl.ds(r0, CHUNK)]

    def state_step(ci, rev, c_ref, cs_ref):
        r0 = pl.multiple_of(ci * CHUNK, CHUNK)
        cs_ref[ci] = c_ref[...].astype(BF16)
        vw = (vt_ref[:, pl.ds(r0, CHUNK)].astype(F32) * gate_row(4, rev, r0)).astype(BF16)
        decay = gate_row(5, rev, r0)[:, 0:1]
        c_ref[...] = decay * c_ref[...] + jnp.dot(vw, ks_ref[pl.ds(r0, CHUNK), :],
                                                  preferred_element_type=F32)

    cf_ref[...] = jnp.zeros_like(cf_ref)
    cb_ref[...] = jnp.zeros_like(cb_ref)

    def state_body(c, carry):
        state_step(c, 0, cf_ref, csf_ref)
        state_step(nc - 1 - c, 1, cb_ref, csb_ref)
        return carry

    lax.fori_loop(0, nc, state_body, 0, unroll=4)

    s_idx = lax.broadcasted_iota(jnp.int32, (LANES, LANES), 0)
    t_idx = lax.broadcasted_iota(jnp.int32, (LANES, LANES), 1)
    tri = (s_idx <= t_idx, s_idx >= t_idx)
    krow = lax.broadcasted_iota(jnp.int32, (BF16_SUBLANES, CHUNK), 0)
    nblk = CHUNK // LANES

    def intra_weights(a, qk, rev):
        cols = []
        for bt in range(nblk):
            col = []
            for bs in range(nblk):
                blk = (slice(bs * LANES, (bs + 1) * LANES), slice(bt * LANES, (bt + 1) * LANES))
                if bs == bt:
                    w = qk[blk] * jnp.exp(jnp.where(tri[rev], a[blk], -jnp.inf))
                elif (bs > bt) == bool(rev):
                    w = qk[blk] * jnp.exp(a[blk])
                else:
                    w = jnp.zeros((LANES, LANES), F32)
                col.append(w.astype(BF16))
            cols.append(jnp.concatenate(col, axis=0))
        return jnp.concatenate(cols, axis=1)

    def split_rows(x, first):
        hi = x.astype(BF16).astype(F32)
        mid = (x - hi).astype(BF16).astype(F32)
        lo = x - hi - mid
        out = jnp.where(krow == first, hi, jnp.where(krow == first + 1, mid,
                                                     jnp.where(krow == first + 2, lo, 0.0)))
        ones = (krow >= 3 - first) & (krow < 6 - first)
        return jnp.where(ones, 1.0, out).astype(BF16)

    dirs = ((0, csf_ref), (1, csb_ref))

    def out_body(g, carry):
        cs = [g * M_OUT_GROUP + i for i in range(M_OUT_GROUP)]
        r0s = [pl.multiple_of(c * CHUNK, CHUNK) for c in cs]
        qts = [qt_ref[:, pl.ds(r0, CHUNK)] for r0 in r0s]
        qks = [jnp.dot(ks_ref[pl.ds(r0, CHUNK), :], qt, preferred_element_type=F32)
               for r0, qt in zip(r0s, qts)]
        a_s = [[lax.dot_general(split_rows(gate_row(0, rev, r0), 0),
                                split_rows(-gate_row(1, rev, r0), 3),
                                (((0,), (0,)), ((), ())), preferred_element_type=F32)
                for rev, _ in dirs] for r0 in r0s]
        inters = [[jnp.dot(cs_ref[c], qt, preferred_element_type=F32) for _, cs_ref in dirs]
                  for c, qt in zip(cs, qts)]
        ws = [[intra_weights(a[rev], qk, rev) for rev, _ in dirs] for a, qk in zip(a_s, qks)]
        tots = [[jnp.dot(vt_ref[:, pl.ds(r0, CHUNK)], w[rev], preferred_element_type=F32)
                 + gate_row(3, rev, r0) * inter[rev] for rev, _ in dirs]
                for r0, w, inter in zip(r0s, ws, inters)]
        for r0, tot in zip(r0s, tots):
            h = None
            for rev, _ in dirs:
                den = jnp.maximum(jnp.abs(tot[rev][dh:dh + 1, :]), jnp.exp(-gate_row(2, rev, r0)))
                hd = tot[rev][:dh, :] * (1.0 / den)
                h = hd if h is None else h + hd
            hc = h - jnp.mean(h, axis=0, keepdims=True)
            hn = hc * lax.rsqrt(jnp.mean(hc * hc, axis=0, keepdims=True) + MH_EPS)
            og = _sigmoid(o_ref[pl.ds(r0, CHUNK), :].astype(F32))
            out_ref[pl.ds(r0, CHUNK), :] = (hn.T * ng_ref[...] * og).astype(BF16)
        return carry

    lax.fori_loop(0, nc // M_OUT_GROUP, out_body, 0)


def _mlstm(layer, u, rows, conv_w, conv_b, norm_g, batch, seq):
    dh = M_HEAD_DIM
    de = dh + M_EXT
    nh = M_HEADS
    nc = seq // CHUNK

    def ucol(base):
        return pl.BlockSpec((seq, dh), lambda b, h: (b, base // dh + h))

    return pl.pallas_call(
        _mlstm_kernel,
        grid=(batch, nh),
        in_specs=[ucol(U_Q), ucol(U_K), ucol(U_V), ucol(U_O),
                  pl.BlockSpec((None, None, GATE_ROWS, seq), lambda b, h: (b, h, 0, 0)),
                  pl.BlockSpec((None, M_CONV, dh), lambda b, h: (layer, 0, h)),
                  pl.BlockSpec((None, 1, dh), lambda b, h: (layer, 0, h)),
                  pl.BlockSpec((None, M_CONV, dh), lambda b, h: (layer, 0, nh + h)),
                  pl.BlockSpec((None, 1, dh), lambda b, h: (layer, 0, nh + h)),
                  pl.BlockSpec((None, 1, dh), lambda b, h: (layer, 0, h))],
        out_specs=pl.BlockSpec((seq, dh), lambda b, h: (b, h)),
        out_shape=jax.ShapeDtypeStruct((batch * seq, M_WIDTH), BF16),
        scratch_shapes=[pltpu.VMEM((dh, seq), BF16),
                        pltpu.VMEM((seq, dh), BF16),
                        pltpu.VMEM((de, seq), BF16),
                        pltpu.VMEM((nc, de, dh), BF16),
                        pltpu.VMEM((nc, de, dh), BF16),
                        pltpu.VMEM((de, dh), F32),
                        pltpu.VMEM((de, dh), F32)],
        compiler_params=pltpu.CompilerParams(dimension_semantics=("arbitrary", "arbitrary"),
                                             vmem_limit_bytes=VMEM_LIMIT),
        name="mlstm",
    )(u, u, u, u, rows, conv_w, conv_b, conv_w, conv_b, norm_g)


A_DELTAS = (-A_BLOCK, 0, -2 * A_BLOCK)


def _attn_kernel(layer, sink_ref, q_ref, kv_ref, out_ref, bias_ref):
    seq = q_ref.shape[0]
    nb = seq // A_BLOCK
    span = 3 * A_BLOCK
    lo = lax.broadcasted_iota(jnp.int32, (A_BLOCK, LANES), 1) < A_HEAD_DIM
    lo_kv = lax.broadcasted_iota(jnp.int32, (span, LANES), 1) < A_HEAD_DIM
    sum_lane = lax.broadcasted_iota(jnp.int32, (span, LANES), 1)
    qi = lax.broadcasted_iota(jnp.int32, (A_BLOCK, span), 0)
    ki = lax.broadcasted_iota(jnp.int32, (A_BLOCK, span), 1)
    for variant, delta in enumerate(A_DELTAS):
        rel = jnp.abs(ki - qi + delta)
        base = jnp.where(rel <= WINDOW, -rel.astype(F32), -jnp.inf)
        for j in range(A_GROUP):
            for half in range(A_KV_HEADS):
                head = j + A_GROUP * half
                bias_ref[variant * A_GROUP + j, :, half * span:(half + 1) * span] = (
                    base * (2.0 ** -(head + 1)))

    def block(n, carry):
        q0 = pl.multiple_of(n * A_BLOCK, A_BLOCK)
        k0 = pl.multiple_of(jnp.clip(q0 - A_BLOCK, 0, seq - span), A_BLOCK)
        variant = jnp.where(n == 0, 1, jnp.where(n == nb - 1, 2, 0))
        k3 = kv_ref[pl.ds(k0, span), 0:A_KV_WIDTH]
        v3 = kv_ref[pl.ds(k0, span), A_KV_WIDTH:2 * A_KV_WIDTH]
        zero = jnp.zeros_like(k3)
        kd = jnp.concatenate([jnp.where(lo_kv, k3, zero), jnp.where(lo_kv, zero, k3)], axis=0)
        vd = jnp.concatenate(
            [jnp.concatenate([jnp.where(lo_kv, v3, zero), (sum_lane == 0).astype(BF16)], axis=1),
             jnp.concatenate([jnp.where(lo_kv, zero, v3), (sum_lane == 1).astype(BF16)], axis=1)],
            axis=0)
        pairs = range(A_GROUP)
        scs = [lax.dot_general(q_ref[pl.ds(q0, A_BLOCK), j * LANES:(j + 1) * LANES], kd,
                               (((1,), (1,)), ((), ())), preferred_element_type=F32)
               + bias_ref[variant * A_GROUP + j] for j in pairs]
        es, corrs = [], []
        for j in pairs:
            e, corr = [], []
            for half in range(A_KV_HEADS):
                sk = sink_ref[layer, j + A_GROUP * half]
                sch = scs[j][:, half * span:(half + 1) * span]
                mx = jnp.maximum(jnp.max(sch, axis=1, keepdims=True), sk)
                e.append(jnp.exp(sch - mx).astype(BF16))
                corr.append(jnp.exp(sk - mx))
            es.append(jnp.concatenate(e, axis=1))
            corrs.append(corr)
        pvs = [jnp.dot(es[j], vd, preferred_element_type=F32) for j in pairs]
        for j in pairs:
            rden = [1.0 / (pvs[j][:, LANES + half:LANES + half + 1] + corrs[j][half])
                    for half in range(A_KV_HEADS)]
            out_ref[pl.ds(q0, A_BLOCK), j * LANES:(j + 1) * LANES] = (
                pvs[j][:, :LANES] * jnp.where(lo, rden[0], rden[1])).astype(BF16)
        return carry

    lax.fori_loop(0, nb, block, 0)


def _attn(layer, u, sink, batch, seq):
    return pl.pallas_call(
        functools.partial(_attn_kernel, layer),
        grid=(batch,),
        in_specs=[pl.BlockSpec(memory_space=pltpu.SMEM),
                  pl.BlockSpec((seq, A_WIDTH), lambda b: (b, U_AQ // A_WIDTH)),
                  pl.BlockSpec((seq, 2 * A_KV_WIDTH), lambda b: (b, U_AKV // (2 * A_KV_WIDTH)))],
        out_specs=pl.BlockSpec((seq, A_WIDTH), lambda b: (b, 0)),
        out_shape=jax.ShapeDtypeStruct((batch * seq, A_WIDTH), BF16),
        scratch_shapes=[pltpu.VMEM((len(A_DELTAS) * A_GROUP, A_BLOCK, 6 * A_BLOCK), F32)],
        compiler_params=pltpu.CompilerParams(dimension_semantics=("arbitrary",),
                                             vmem_limit_bytes=VMEM_LIMIT),
        name="attn",
    )(sink, u, u)


def _ff_chunks(d_ff):
    step = 1024
    return [(c0, min(c0 + step, d_ff)) for c0 in range(0, d_ff, step)]


def _post_kernel(alpha, x_ref, hm_ref, ha_ref, p_ref, wom_ref, woa_ref, wg_ref, wu_ref, wd_ref,
                 wpg_ref, wpp_ref, ln1g_ref, ln1b_ref, ln2g_ref, ln2b_ref, y_ref):
    subs = [slice(r0, r0 + POST_SUB) for r0 in range(0, x_ref.shape[0], POST_SUB)]
    x1s = []
    for rows in subs:
        mix = jnp.dot(hm_ref[rows, :], wom_ref[...], preferred_element_type=F32)
        mix = mix + jnp.dot(ha_ref[rows, :], woa_ref[...], preferred_element_type=F32)
        x1s.append(_layer_norm(alpha * x_ref[rows, :] + mix, ln1g_ref[...], ln1b_ref[...]))
    accs = []
    for rows, x1 in zip(subs, x1s):
        x1b = x1.astype(BF16)
        ple_gate = _sigmoid(jnp.dot(x1b, wpg_ref[...], preferred_element_type=F32))
        acc = alpha * x1 + ple_gate * jnp.dot(p_ref[rows, :].astype(BF16), wpp_ref[...],
                                              preferred_element_type=F32)
        for c0, c1 in _ff_chunks(wd_ref.shape[0]):
            gate = jnp.dot(x1b, wg_ref[:, c0:c1], preferred_element_type=F32)
            up = jnp.dot(x1b, wu_ref[:, c0:c1], preferred_element_type=F32)
            hid = (gate * _sigmoid(gate) * up).astype(BF16)
            acc = acc + jnp.dot(hid, wd_ref[c0:c1, :], preferred_element_type=F32)
        accs.append(acc)
    for rows, acc in zip(subs, accs):
        y_ref[rows, :] = _layer_norm(acc, ln2g_ref[...], ln2b_ref[...])


def _post(layer, alpha, x2, hm, ha, p3, wom, woa, wg, wu, wd, wpg, wpp, ln1g, ln1b, ln2g, ln2b):
    t, d = x2.shape
    tm = POST_TM

    def rows(width):
        return pl.BlockSpec((tm, width), lambda i: (i, 0))

    weights = [wom, woa, wg, wu, wd, wpg, wpp, ln1g, ln1b, ln2g, ln2b]
    return pl.pallas_call(
        functools.partial(_post_kernel, alpha),
        grid=(t // tm,),
        in_specs=[rows(d), rows(hm.shape[1]), rows(ha.shape[1]),
                  pl.BlockSpec((None, tm, p3.shape[2]), lambda i: (layer, i, 0))]
                 + [_layer_spec(w, layer) for w in weights],
        out_specs=rows(d),
        out_shape=jax.ShapeDtypeStruct((t, d), F32),
        compiler_params=pltpu.CompilerParams(dimension_semantics=("arbitrary",),
                                             vmem_limit_bytes=VMEM_LIMIT),
        name="post",
    )(x2, hm, ha, p3, *weights)


def _pair_heads(w, axis):
    shape = w.shape
    w = w.reshape(shape[:axis] + (A_KV_HEADS, A_GROUP, A_HEAD_DIM) + shape[axis + 1:])
    return jnp.swapaxes(w, axis, axis + 1).reshape(shape)


def kernel(x, p, w_in, b_gate, conv_w, conv_b, mlstm_norm_g, attn_sink, w_out, ln1_g, ln1_b,
           w_ffn_in, w_ffn_out, ln2_g, ln2_b, w_ple_gate, w_ple_proj):
    batch, seq, d = x.shape
    depth = w_in.shape[0]
    d_ff = w_ffn_out.shape[1]
    t = batch * seq
    alpha = float((2 * depth) ** 0.25)
    assert seq % (CHUNK * M_OUT_GROUP) == 0 and seq % A_BLOCK == 0 and seq >= 3 * A_BLOCK
    assert t % INPROJ_TM == 0 and t % POST_TM == 0
    assert w_in.shape[2] == U_WIDTH + N_GATE

    g0 = U_MAIN
    aq0 = g0 + N_GATE
    akv0 = aq0 + A_WIDTH
    wm = w_in[:, :, :g0].astype(BF16)
    waq = (_pair_heads(w_in[:, :, aq0:akv0], 2) * (A_HEAD_DIM ** -0.5)).astype(BF16)
    wakv = w_in[:, :, akv0:].astype(BF16)
    wgate = jnp.pad(w_in[:, :, g0:aq0], ((0, 0), (0, 0), (0, G_WIDTH - N_GATE))).astype(BF16)
    gate_bias = jnp.pad(b_gate, ((0, 0), (0, G_WIDTH - N_GATE)))[:, None, :]
    wom = w_out[:, :M_WIDTH].astype(BF16)
    woa = _pair_heads(w_out[:, M_WIDTH:], 1).astype(BF16)
    wg = w_ffn_in[:, :, :d_ff].astype(BF16)
    wu = w_ffn_in[:, :, d_ff:].astype(BF16)
    wd = w_ffn_out.astype(BF16)
    wpg = w_ple_gate.astype(BF16)
    wpp = w_ple_proj.astype(BF16)
    conv_b3, norm_g3, ln1_g3, ln1_b3, ln2_g3, ln2_b3 = (
        a[:, None, :] for a in (conv_b, mlstm_norm_g, ln1_g, ln1_b, ln2_g, ln2_b))

    h = x.reshape(t, d)
    p3 = p.reshape(depth, t, p.shape[-1])
    for i in range(depth):
        u, g = _inproj(i, h, wm, waq, wakv, wgate)
        rows = _gateprep(i, g, gate_bias, batch, seq)
        hm = _mlstm(i, u, rows, conv_w, conv_b3, norm_g3, batch, seq)
        ha = _attn(i, u, attn_sink, batch, seq)
        h = _post(i, alpha, h, hm, ha, p3, wom, woa, wg, wu, wd, wpg, wpp,
                  ln1_g3, ln1_b3, ln2_g3, ln2_b3)
    return h.reshape(batch, seq, d)
```

```python
import functools

import jax
import jax.numpy as jnp
from jax import lax
from jax.experimental import pallas as pl
from jax.experimental.pallas import tpu as pltpu

F32 = jnp.float32
BF16 = jnp.bfloat16

M_HEADS = 4
M_HEAD_DIM = 128
M_WIDTH = M_HEADS * M_HEAD_DIM
M_CONV = 5
N_GATE = 4 * M_HEADS
A_HEADS = 8
A_KV_HEADS = 2
A_HEAD_DIM = 64
A_GROUP = A_HEADS // A_KV_HEADS
A_WIDTH = A_HEADS * A_HEAD_DIM
A_KV_WIDTH = A_KV_HEADS * A_HEAD_DIM
WINDOW = 128
LN_EPS = 1e-5
MH_EPS = 1e-6

LANES = 128
F32_SUBLANES = 8
BF16_SUBLANES = 16

CHUNK = 256
A_BLOCK = 128
U_K, U_O = 0, M_WIDTH
U_AQ = 2 * M_WIDTH
U_AKV = U_AQ + A_WIDTH
U_WIDTH = U_AKV + 2 * A_KV_WIDTH
G_WIDTH = LANES
INPROJ_COLS = 256
INPROJ_STRIP = 128
GATE_ROWS = 16
M_OUT_GROUP = 4
M_EXT = BF16_SUBLANES
INPROJ_TM = 1024
POST_TM = 512
POST_SUB = 256
VMEM_LIMIT = 56 * 1024 * 1024


def _sigmoid(x):
    return 1.0 / (1.0 + jnp.exp(-x))


def _layer_norm(z, g, b):
    mu = jnp.mean(z, axis=-1, keepdims=True)
    zc = z - mu
    var = jnp.mean(zc * zc, axis=-1, keepdims=True)
    return zc * lax.rsqrt(var + LN_EPS) * g + b


def _layer_spec(stacked, layer):
    _, rows, cols = stacked.shape
    return pl.BlockSpec((None, rows, cols), lambda *_: (layer, 0, 0), pipeline_mode=pl.Buffered(1))


def _inproj_kernel(seq, x_ref, xp_ref, xn_ref, wqk_ref, wv_ref, wo_ref, waq_ref, wakv_ref, wg_ref,
                   cw_ref, cb_ref, qt_ref, vt_ref, u_ref, g_ref, win_ref):
    tm = x_ref.shape[0]
    halo = F32_SUBLANES
    tiles_per_seq = seq // tm
    pos = pl.program_id(0) % tiles_per_seq
    xe = jnp.concatenate([x_ref[...], xp_ref[...], xn_ref[...]], axis=0).astype(BF16)
    xb = xe[:tm]
    cw = cw_ref[...]
    cb = cb_ref[...]

    def plain(dst, w_ref, w_cols):
        def run():
            dst[...] = jnp.dot(xb, w_ref[:, w_cols], preferred_element_type=F32).astype(dst.dtype)
        return run

    def conv_cols(slot, c0):
        cols = slice(c0, c0 + INPROJ_COLS)
        win = win_ref.at[slot]
        pre = jnp.dot(xe, wqk_ref[:, cols], preferred_element_type=F32)
        win[0:halo, :] = jnp.where(pos > 0, pre[tm:tm + halo], 0.0)
        win[halo:halo + tm, :] = pre[:tm]
        win[halo + tm:, :] = jnp.where(pos < tiles_per_seq - 1, pre[tm + halo:], 0.0)
        for r0 in range(0, tm, INPROJ_STRIP):
            rows = slice(r0, r0 + INPROJ_STRIP)
            y = cb[:, cols]
            for k in range(M_CONV):
                first = r0 + halo + k - M_CONV // 2
                y = y + cw[k:k + 1, cols] * win[first:first + INPROJ_STRIP, :]
            y = y * _sigmoid(y)
            if c0 < M_WIDTH:
                qt_ref[cols, rows] = y.T.astype(BF16)
            else:
                u_ref[rows, U_K + c0 - M_WIDTH:U_K + c0 - M_WIDTH + INPROJ_COLS] = (
                    y * (M_HEAD_DIM ** -0.5)).astype(BF16)

    def v_cols(slot, c0):
        cols = slice(c0, c0 + INPROJ_COLS)
        win = win_ref.at[slot]
        win[0:tm, :] = jnp.dot(xb, wv_ref[:, cols], preferred_element_type=F32)
        for r0 in range(0, tm, INPROJ_STRIP):
            rows = slice(r0, r0 + INPROJ_STRIP)
            vt_ref[cols, rows] = win[rows, :].T.astype(BF16)

    half = (U_AQ - U_O) // 2
    plains = [plain(u_ref.at[:, U_O:U_O + half], wo_ref, slice(0, half)),
              plain(u_ref.at[:, U_O + half:U_AQ], wo_ref, slice(half, 2 * half)),
              plain(u_ref.at[:, U_AQ:U_AQ + half], waq_ref, slice(0, half)),
              plain(u_ref.at[:, U_AQ + half:U_AKV], waq_ref, slice(half, 2 * half)),
              plain(u_ref.at[:, U_AKV:], wakv_ref, slice(None)),
              plain(g_ref, wg_ref, slice(None))]
    passes = ([functools.partial(conv_cols, i % 2, c0)
               for i, c0 in enumerate(range(0, 2 * M_WIDTH, INPROJ_COLS))]
              + [functools.partial(v_cols, i % 2, c0)
                 for i, c0 in enumerate(range(0, M_WIDTH, INPROJ_COLS))])
    for run_pass, run_plain in zip(passes, plains):
        run_pass()
        run_plain()


def _inproj(layer, x2, seq, wqk, wv, wo, waq, wakv, wg, conv_w, conv_b):
    t, d = x2.shape
    tm = INPROJ_TM
    halo = F32_SUBLANES
    weights = [wqk, wv, wo, waq, wakv, wg, conv_w, conv_b]
    return pl.pallas_call(
        functools.partial(_inproj_kernel, seq),
        grid=(t // tm,),
        in_specs=[pl.BlockSpec((tm, d), lambda i: (i, 0)),
                  pl.BlockSpec((halo, d), lambda i: (jnp.maximum(i * (tm // halo) - 1, 0), 0)),
                  pl.BlockSpec((halo, d),
                               lambda i: (jnp.minimum((i + 1) * (tm // halo), t // halo - 1), 0))]
                 + [_layer_spec(w, layer) for w in weights],
        out_specs=[pl.BlockSpec((M_WIDTH, tm), lambda i: (0, i)),
                   pl.BlockSpec((M_WIDTH, tm), lambda i: (0, i)),
                   pl.BlockSpec((tm, U_WIDTH), lambda i: (i, 0)),
                   pl.BlockSpec((tm, G_WIDTH), lambda i: (i, 0))],
        out_shape=[jax.ShapeDtypeStruct((M_WIDTH, t), BF16),
                   jax.ShapeDtypeStruct((M_WIDTH, t), BF16),
                   jax.ShapeDtypeStruct((t, U_WIDTH), BF16),
                   jax.ShapeDtypeStruct((t, G_WIDTH), F32)],
        scratch_shapes=[pltpu.VMEM((2, tm + 2 * halo, INPROJ_COLS), F32)],
        compiler_params=pltpu.CompilerParams(dimension_semantics=("arbitrary",),
                                             vmem_limit_bytes=VMEM_LIMIT),
        name="inproj",
    )(x2, x2, x2, *weights)


def _gateprep_kernel(g_ref, bias_ref, rows_ref, gt_ref, mp_ref):
    s = g_ref.shape[0]
    nc = s // CHUNK
    nd = 2 * M_HEADS

    def transpose_tile(i, carry):
        r0 = pl.multiple_of(i * LANES, LANES)
        tile = g_ref[pl.ds(r0, LANES), :] + bias_ref[...]
        gt_ref[:, pl.ds(r0, LANES)] = tile.T[:N_GATE, :]
        return carry

    lax.fori_loop(0, s // LANES, transpose_tile, 0)
    gt = gt_ref[...]
    ig = gt[0:nd]
    fg = gt[nd:2 * nd]
    lf = jnp.minimum(fg, 0.0) - jnp.log1p(jnp.exp(-jnp.abs(fg)))
    pos = lax.broadcasted_iota(jnp.int32, (nd, s), 1) & (CHUNK - 1)
    fwd_row = lax.broadcasted_iota(jnp.int32, (nd, s), 0) < M_HEADS

    def chunk_scan(x, op, identity):
        pre, suf = x, x
        sh = 1
        while sh < CHUNK:
            pre = op(pre, jnp.where(pos >= sh, pltpu.roll(pre, sh, 1), identity))
            suf = op(suf, jnp.where(pos < CHUNK - sh, pltpu.roll(suf, s - sh, 1), identity))
            sh *= 2
        return jnp.where(fwd_row, pre, suf)

    bcum = chunk_scan(lf, jnp.add, 0.0)
    r = ig - bcum
    rcummax = chunk_scan(r, jnp.maximum, -jnp.inf)

    def chunk_allreduce(x, op):
        sh = 1
        while sh < CHUNK:
            partner = jnp.where((pos & sh) != 0, pltpu.roll(x, sh, 1), pltpu.roll(x, s - sh, 1))
            x = op(x, partner)
            sh *= 2
        return x

    gtot = chunk_allreduce(lf, jnp.add)
    rmax = chunk_allreduce(r, jnp.maximum)
    m = jnp.zeros((M_HEADS, CHUNK), F32)
    for c in range(nc):
        sl = slice(c * CHUNK, (c + 1) * CHUNK)
        mp_ref[0:M_HEADS, sl] = m
        m = gtot[0:M_HEADS, sl] + jnp.maximum(m, rmax[0:M_HEADS, sl])
    m = jnp.zeros((M_HEADS, CHUNK), F32)
    for c in reversed(range(nc)):
        sl = slice(c * CHUNK, (c + 1) * CHUNK)
        mp_ref[M_HEADS:nd, sl] = m
        m = gtot[M_HEADS:nd, sl] + jnp.maximum(m, rmax[M_HEADS:nd, sl])
    m_in = mp_ref[...]
    mm = jnp.maximum(m_in, rmax)
    mu = jnp.maximum(rcummax, m_in)
    table = (r, mu, bcum + mu, jnp.exp(m_in - mu), jnp.exp(r - mm), jnp.exp(m_in - mm))
    pad = jnp.zeros((GATE_ROWS - 2 * len(table), s), F32)
    for h in range(M_HEADS):
        pieces = []
        for arr in table:
            pieces += [arr[h:h + 1], arr[M_HEADS + h:M_HEADS + h + 1]]
        rows_ref[h] = jnp.concatenate(pieces + [pad], axis=0)


def _gateprep(layer, g2, bias, batch, seq):
    return pl.pallas_call(
        _gateprep_kernel,
        grid=(batch,),
        in_specs=[pl.BlockSpec((seq, G_WIDTH), lambda b: (b, 0)),
                  pl.BlockSpec((None, 1, G_WIDTH), lambda b: (layer, 0, 0))],
        out_specs=pl.BlockSpec((None, M_HEADS, GATE_ROWS, seq), lambda b: (b, 0, 0, 0)),
        out_shape=jax.ShapeDtypeStruct((batch, M_HEADS, GATE_ROWS, seq), F32),
        scratch_shapes=[pltpu.VMEM((N_GATE, seq), F32),
                        pltpu.VMEM((2 * M_HEADS, seq), F32)],
        compiler_params=pltpu.CompilerParams(dimension_semantics=("arbitrary",),
                                             vmem_limit_bytes=VMEM_LIMIT),
        name="gateprep",
    )(g2, bias)


def _mlstm_kernel(qt_ref, vin_ref, ks_ref, o_ref, rows_ref, ng_ref, out_ref,
                  vt_ref, csf_ref, csb_ref, cf_ref, cb_ref):
    seq = ks_ref.shape[0]
    nc = seq // CHUNK
    dh = M_HEAD_DIM
    de = dh + M_EXT

    vt_ref[0:dh, :] = vin_ref[...]
    vt_ref[dh:de, :] = (lax.broadcasted_iota(jnp.int32, (M_EXT, seq), 0) == 0).astype(BF16)

    def gate_row(kind, rev, r0):
        return rows_ref[2 * kind + rev:2 * kind + rev + 1, pl.ds(r0, CHUNK)]

    def state_step(ci, rev, c_ref, cs_ref):
        r0 = pl.multiple_of(ci * CHUNK, CHUNK)
        cs_ref[ci] = c_ref[...].astype(BF16)
        vw = (vt_ref[:, pl.ds(r0, CHUNK)].astype(F32) * gate_row(4, rev, r0)).astype(BF16)
        decay = gate_row(5, rev, r0)[:, 0:1]
        c_ref[...] = decay * c_ref[...] + jnp.dot(vw, ks_ref[pl.ds(r0, CHUNK), :],
                                                  preferred_element_type=F32)

    cf_ref[...] = jnp.zeros_like(cf_ref)
    cb_ref[...] = jnp.zeros_like(cb_ref)

    def state_body(c, carry):
        state_step(c, 0, cf_ref, csf_ref)
        state_step(nc - 1 - c, 1, cb_ref, csb_ref)
        return carry

    lax.fori_loop(0, nc, state_body, 0, unroll=4)

    s_idx = lax.broadcasted_iota(jnp.int32, (LANES, LANES), 0)
    t_idx = lax.broadcasted_iota(jnp.int32, (LANES, LANES), 1)
    tri = (s_idx <= t_idx, s_idx >= t_idx)
    krow = lax.broadcasted_iota(jnp.int32, (BF16_SUBLANES, CHUNK), 0)
    nblk = CHUNK // LANES

    def intra_weights(a, qk, rev):
        cols = []
        for bt in range(nblk):
            col = []
            for bs in range(nblk):
                blk = (slice(bs * LANES, (bs + 1) * LANES), slice(bt * LANES, (bt + 1) * LANES))
                if bs == bt:
                    w = qk[blk] * jnp.exp(jnp.where(tri[rev], a[blk], -jnp.inf))
                elif (bs > bt) == bool(rev):
                    w = qk[blk] * jnp.exp(a[blk])
                else:
                    w = jnp.zeros((LANES, LANES), F32)
                col.append(w.astype(BF16))
            cols.append(jnp.concatenate(col, axis=0))
        return jnp.concatenate(cols, axis=1)

    def split_rows(x, first):
        hi = x.astype(BF16).astype(F32)
        mid = (x - hi).astype(BF16).astype(F32)
        lo = x - hi - mid
        out = jnp.where(krow == first, hi, jnp.where(krow == first + 1, mid,
                                                     jnp.where(krow == first + 2, lo, 0.0)))
        ones = (krow >= 3 - first) & (krow < 6 - first)
        return jnp.where(ones, 1.0, out).astype(BF16)

    dirs = ((0, csf_ref), (1, csb_ref))

    def out_body(g, carry):
        cs = [g * M_OUT_GROUP + i for i in range(M_OUT_GROUP)]
        r0s = [pl.multiple_of(c * CHUNK, CHUNK) for c in cs]
        qts = [qt_ref[:, pl.ds(r0, CHUNK)] for r0 in r0s]
        qks = [jnp.dot(ks_ref[pl.ds(r0, CHUNK), :], qt, preferred_element_type=F32)
               for r0, qt in zip(r0s, qts)]
        a_s = [[lax.dot_general(split_rows(gate_row(0, rev, r0), 0),
                                split_rows(-gate_row(1, rev, r0), 3),
                                (((0,), (0,)), ((), ())), preferred_element_type=F32)
                for rev, _ in dirs] for r0 in r0s]
        inters = [[jnp.dot(cs_ref[c], qt, preferred_element_type=F32) for _, cs_ref in dirs]
                  for c, qt in zip(cs, qts)]
        ws = [[intra_weights(a[rev], qk, rev) for rev, _ in dirs] for a, qk in zip(a_s, qks)]
        tots = [[jnp.dot(vt_ref[:, pl.ds(r0, CHUNK)], w[rev], preferred_element_type=F32)
                 + gate_row(3, rev, r0) * inter[rev] for rev, _ in dirs]
                for r0, w, inter in zip(r0s, ws, inters)]
        for r0, tot in zip(r0s, tots):
            h = None
            for rev, _ in dirs:
                den = jnp.maximum(jnp.abs(tot[rev][dh:dh + 1, :]), jnp.exp(-gate_row(2, rev, r0)))
                hd = tot[rev][:dh, :] * (1.0 / den)
                h = hd if h is None else h + hd
            hc = h - jnp.mean(h, axis=0, keepdims=True)
            hn = hc * lax.rsqrt(jnp.mean(hc * hc, axis=0, keepdims=True) + MH_EPS)
            og = _sigmoid(o_ref[pl.ds(r0, CHUNK), :].astype(F32))
            out_ref[pl.ds(r0, CHUNK), :] = (hn.T * ng_ref[...] * og).astype(BF16)
        return carry

    lax.fori_loop(0, nc // M_OUT_GROUP, out_body, 0)


def _mlstm(layer, qt, vt, u, rows, norm_g, batch, seq):
    dh = M_HEAD_DIM
    de = dh + M_EXT
    nc = seq // CHUNK

    def tcol():
        return pl.BlockSpec((dh, seq), lambda b, h: (h, b))

    def ucol(base):
        return pl.BlockSpec((seq, dh), lambda b, h: (b, base // dh + h))

    return pl.pallas_call(
        _mlstm_kernel,
        grid=(batch, M_HEADS),
        in_specs=[tcol(), tcol(), ucol(U_K), ucol(U_O),
                  pl.BlockSpec((None, None, GATE_ROWS, seq), lambda b, h: (b, h, 0, 0)),
                  pl.BlockSpec((None, 1, dh), lambda b, h: (layer, 0, h))],
        out_specs=pl.BlockSpec((seq, dh), lambda b, h: (b, h)),
        out_shape=jax.ShapeDtypeStruct((batch * seq, M_WIDTH), BF16),
        scratch_shapes=[pltpu.VMEM((de, seq), BF16),
                        pltpu.VMEM((nc, de, dh), BF16),
                        pltpu.VMEM((nc, de, dh), BF16),
                        pltpu.VMEM((de, dh), F32),
                        pltpu.VMEM((de, dh), F32)],
        compiler_params=pltpu.CompilerParams(dimension_semantics=("arbitrary", "arbitrary"),
                                             vmem_limit_bytes=VMEM_LIMIT),
        name="mlstm",
    )(qt, vt, u, u, rows, norm_g)


A_DELTAS = (-A_BLOCK, 0, -2 * A_BLOCK)


def _attn_kernel(layer, sink_ref, q_ref, kv_ref, out_ref, bias_ref):
    seq = q_ref.shape[0]
    nb = seq // A_BLOCK
    span = 3 * A_BLOCK
    lo = lax.broadcasted_iota(jnp.int32, (A_BLOCK, LANES), 1) < A_HEAD_DIM
    lo_kv = lax.broadcasted_iota(jnp.int32, (span, LANES), 1) < A_HEAD_DIM
    sum_lane = lax.broadcasted_iota(jnp.int32, (span, LANES), 1)
    qi = lax.broadcasted_iota(jnp.int32, (A_BLOCK, span), 0)
    ki = lax.broadcasted_iota(jnp.int32, (A_BLOCK, span), 1)
    for variant, delta in enumerate(A_DELTAS):
        rel = jnp.abs(ki - qi + delta)
        base = jnp.where(rel <= WINDOW, -rel.astype(F32), -jnp.inf)
        for j in range(A_GROUP):
            for half in range(A_KV_HEADS):
                head = j + A_GROUP * half
                bias_ref[variant * A_GROUP + j, :, half * span:(half + 1) * span] = (
                    base * (2.0 ** -(head + 1)))

    def block(n, carry):
        q0 = pl.multiple_of(n * A_BLOCK, A_BLOCK)
        k0 = pl.multiple_of(jnp.clip(q0 - A_BLOCK, 0, seq - span), A_BLOCK)
        variant = jnp.where(n == 0, 1, jnp.where(n == nb - 1, 2, 0))
        k3 = kv_ref[pl.ds(k0, span), 0:A_KV_WIDTH]
        v3 = kv_ref[pl.ds(k0, span), A_KV_WIDTH:2 * A_KV_WIDTH]
        zero = jnp.zeros_like(k3)
        kd = jnp.concatenate([jnp.where(lo_kv, k3, zero), jnp.where(lo_kv, zero, k3)], axis=0)
        vd = jnp.concatenate(
            [jnp.concatenate([jnp.where(lo_kv, v3, zero), (sum_lane == 0).astype(BF16)], axis=1),
             jnp.concatenate([jnp.where(lo_kv, zero, v3), (sum_lane == 1).astype(BF16)], axis=1)],
            axis=0)
        pairs = range(A_GROUP)
        scs = [lax.dot_general(q_ref[pl.ds(q0, A_BLOCK), j * LANES:(j + 1) * LANES], kd,
                               (((1,), (1,)), ((), ())), preferred_element_type=F32)
               + bias_ref[variant * A_GROUP + j] for j in pairs]
        es, corrs = [], []
        for j in pairs:
            e, corr = [], []
            for half in range(A_KV_HEADS):
                sk = sink_ref[layer, j + A_GROUP * half]
                sch = scs[j][:, half * span:(half + 1) * span]
                mx = jnp.maximum(jnp.max(sch, axis=1, keepdims=True), sk)
                e.append(jnp.exp(sch - mx).astype(BF16))
                corr.append(jnp.exp(sk - mx))
            es.append(jnp.concatenate(e, axis=1))
            corrs.append(corr)
        pvs = [jnp.dot(es[j], vd, preferred_element_type=F32) for j in pairs]
        for j in pairs:
            rden = [1.0 / (pvs[j][:, LANES + half:LANES + half + 1] + corrs[j][half])
                    for half in range(A_KV_HEADS)]
            out_ref[pl.ds(q0, A_BLOCK), j * LANES:(j + 1) * LANES] = (
                pvs[j][:, :LANES] * jnp.where(lo, rden[0], rden[1])).astype(BF16)
        return carry

    lax.fori_loop(0, nb, block, 0)


def _attn(layer, u, sink, batch, seq):
    return pl.pallas_call(
        functools.partial(_attn_kernel, layer),
        grid=(batch,),
        in_specs=[pl.BlockSpec(memory_space=pltpu.SMEM),
                  pl.BlockSpec((seq, A_WIDTH), lambda b: (b, U_AQ // A_WIDTH)),
                  pl.BlockSpec((seq, 2 * A_KV_WIDTH), lambda b: (b, U_AKV // (2 * A_KV_WIDTH)))],
        out_specs=pl.BlockSpec((seq, A_WIDTH), lambda b: (b, 0)),
        out_shape=jax.ShapeDtypeStruct((batch * seq, A_WIDTH), BF16),
        scratch_shapes=[pltpu.VMEM((len(A_DELTAS) * A_GROUP, A_BLOCK, 6 * A_BLOCK), F32)],
        compiler_params=pltpu.CompilerParams(dimension_semantics=("arbitrary",),
                                             vmem_limit_bytes=VMEM_LIMIT),
        name="attn",
    )(sink, u, u)


def _ff_chunks(d_ff):
    step = 1024
    return [(c0, min(c0 + step, d_ff)) for c0 in range(0, d_ff, step)]


def _post_kernel(alpha, x_ref, hm_ref, ha_ref, p_ref, wom_ref, woa_ref, wg_ref, wu_ref, wd_ref,
                 wpg_ref, wpp_ref, ln1g_ref, ln1b_ref, ln2g_ref, ln2b_ref, y_ref):
    subs = [slice(r0, r0 + POST_SUB) for r0 in range(0, x_ref.shape[0], POST_SUB)]
    x1s = []
    for rows in subs:
        mix = jnp.dot(hm_ref[rows, :], wom_ref[...], preferred_element_type=F32)
        mix = mix + jnp.dot(ha_ref[rows, :], woa_ref[...], preferred_element_type=F32)
        x1s.append(_layer_norm(alpha * x_ref[rows, :] + mix, ln1g_ref[...], ln1b_ref[...]))
    accs = []
    for rows, x1 in zip(subs, x1s):
        x1b = x1.astype(BF16)
        ple_gate = _sigmoid(jnp.dot(x1b, wpg_ref[...], preferred_element_type=F32))
        acc = alpha * x1 + ple_gate * jnp.dot(p_ref[rows, :].astype(BF16), wpp_ref[...],
                                              preferred_element_type=F32)
        for c0, c1 in _ff_chunks(wd_ref.shape[0]):
            gate = jnp.dot(x1b, wg_ref[:, c0:c1], preferred_element_type=F32)
            up = jnp.dot(x1b, wu_ref[:, c0:c1], preferred_element_type=F32)
            hid = (gate * _sigmoid(gate) * up).astype(BF16)
            acc = acc + jnp.dot(hid, wd_ref[c0:c1, :], preferred_element_type=F32)
        accs.append(acc)
    for rows, acc in zip(subs, accs):
        y_ref[rows, :] = _layer_norm(acc, ln2g_ref[...], ln2b_ref[...])


def _post(layer, alpha, x2, hm, ha, p3, wom, woa, wg, wu, wd, wpg, wpp, ln1g, ln1b, ln2g, ln2b):
    t, d = x2.shape
    tm = POST_TM

    def rows(width):
        return pl.BlockSpec((tm, width), lambda i: (i, 0))

    weights = [wom, woa, wg, wu, wd, wpg, wpp, ln1g, ln1b, ln2g, ln2b]
    return pl.pallas_call(
        functools.partial(_post_kernel, alpha),
        grid=(t // tm,),
        in_specs=[rows(d), rows(hm.shape[1]), rows(ha.shape[1]),
                  pl.BlockSpec((None, tm, p3.shape[2]), lambda i: (layer, i, 0))]
                 + [_layer_spec(w, layer) for w in weights],
        out_specs=rows(d),
        out_shape=jax.ShapeDtypeStruct((t, d), F32),
        compiler_params=pltpu.CompilerParams(dimension_semantics=("arbitrary",),
                                             vmem_limit_bytes=VMEM_LIMIT),
        name="post",
    )(x2, hm, ha, p3, *weights)


def _pair_heads(w, axis):
    shape = w.shape
    w = w.reshape(shape[:axis] + (A_KV_HEADS, A_GROUP, A_HEAD_DIM) + shape[axis + 1:])
    return jnp.swapaxes(w, axis, axis + 1).reshape(shape)


def kernel(x, p, w_in, b_gate, conv_w, conv_b, mlstm_norm_g, attn_sink, w_out, ln1_g, ln1_b,
           w_ffn_in, w_ffn_out, ln2_g, ln2_b, w_ple_gate, w_ple_proj):
    batch, seq, d = x.shape
    depth = w_in.shape[0]
    d_ff = w_ffn_out.shape[1]
    t = batch * seq
    alpha = float((2 * depth) ** 0.25)
    assert seq % (CHUNK * M_OUT_GROUP) == 0 and seq % A_BLOCK == 0 and seq >= 3 * A_BLOCK
    assert t % INPROJ_TM == 0 and t % POST_TM == 0
    assert seq % INPROJ_TM == 0
    assert w_in.shape[2] == 4 * M_WIDTH + N_GATE + A_WIDTH + 2 * A_KV_WIDTH

    g0 = 4 * M_WIDTH
    aq0 = g0 + N_GATE
    akv0 = aq0 + A_WIDTH
    wqk = w_in[:, :, :2 * M_WIDTH].astype(BF16)
    wv = w_in[:, :, 2 * M_WIDTH:3 * M_WIDTH].astype(BF16)
    wo = w_in[:, :, 3 * M_WIDTH:g0].astype(BF16)
    waq =(_pair_heads(w_in[:, :, aq0:akv0], 2) * (A_HEAD_DIM ** -0.5)).astype(BF16)
    wakv = w_in[:, :, akv0:].astype(BF16)
    wgate = jnp.pad(w_in[:, :, g0:aq0], ((0, 0), (0, 0), (0, G_WIDTH - N_GATE))).astype(BF16)
    gate_bias = jnp.pad(b_gate, ((0, 0), (0, G_WIDTH - N_GATE)))[:, None, :]
    wom = w_out[:, :M_WIDTH].astype(BF16)
    woa = _pair_heads(w_out[:, M_WIDTH:], 1).astype(BF16)
    wg = w_ffn_in[:, :, :d_ff].astype(BF16)
    wu = w_ffn_in[:, :, d_ff:].astype(BF16)
    wd = w_ffn_out.astype(BF16)
    wpg = w_ple_gate.astype(BF16)
    wpp = w_ple_proj.astype(BF16)
    conv_b3, norm_g3, ln1_g3, ln1_b3, ln2_g3, ln2_b3 = (
        a[:, None, :] for a in (conv_b, mlstm_norm_g, ln1_g, ln1_b, ln2_g, ln2_b))

    h = x.reshape(t, d)
    p3 = p.reshape(depth, t, p.shape[-1])
    for i in range(depth):
        qt, vt, u, g = _inproj(i, h, seq, wqk, wv, wo, waq, wakv, wgate, conv_w, conv_b3)
        rows = _gateprep(i, g, gate_bias, batch, seq)
        hm = _mlstm(i, qt, vt, u, rows, norm_g3, batch, seq)
        ha = _attn(i, u, attn_sink, batch, seq)
        h = _post(i, alpha, h, hm, ha, p3, wom, woa, wg, wu, wd, wpg, wpp,
                  ln1_g3, ln1_b3, ln2_g3, ln2_b3)
    return h.reshape(batch, seq, d)
```

```python
import functools

import jax
import jax.numpy as jnp
from jax import lax
from jax.experimental import pallas as pl
from jax.experimental.pallas import tpu as pltpu

F32 = jnp.float32
BF16 = jnp.bfloat16

M_HEADS = 4
M_HEAD_DIM = 128
M_WIDTH = M_HEADS * M_HEAD_DIM
M_CONV = 5
N_GATE = 4 * M_HEADS
A_HEADS = 8
A_KV_HEADS = 2
A_HEAD_DIM = 64
A_GROUP = A_HEADS // A_KV_HEADS
A_WIDTH = A_HEADS * A_HEAD_DIM
A_KV_WIDTH = A_KV_HEADS * A_HEAD_DIM
WINDOW = 128
LN_EPS = 1e-5
MH_EPS = 1e-6

LANES = 128
F32_SUBLANES = 8
BF16_SUBLANES = 16

CHUNK = 256
A_BLOCK = 128
A_BLOCKS_PER_ITER = 4
U_K, U_O = 0, M_WIDTH
U_AQ = 2 * M_WIDTH
U_AKV = U_AQ + A_WIDTH
U_WIDTH = U_AKV + 2 * A_KV_WIDTH
G_WIDTH = LANES
INPROJ_COLS = 256
INPROJ_STRIP = 128
GATE_ROWS = 16
M_OUT_GROUP = 4
M_EXT = BF16_SUBLANES
INPROJ_TM = 1024
POST_TM = 512
POST_SUB = 256
VMEM_LIMIT = 56 * 1024 * 1024


def _sigmoid(x):
    return 1.0 / (1.0 + jnp.exp(-x))


def _layer_norm(z, g, b):
    mu = jnp.mean(z, axis=-1, keepdims=True)
    zc = z - mu
    var = jnp.mean(zc * zc, axis=-1, keepdims=True)
    return zc * lax.rsqrt(var + LN_EPS) * g + b


def _layer_spec(stacked, layer, cols=None, col_block=0):
    _, rows, all_cols = stacked.shape
    return pl.BlockSpec((None, rows, cols or all_cols), lambda *_: (layer, 0, col_block),
                        pipeline_mode=pl.Buffered(1))


def _inproj_kernel(seq, x_ref, xp_ref, xn_ref, wqk_ref, wv_ref, wo_ref, waq_ref, wakv_ref, wg_ref,
                   cw_ref, cb_ref, qt_ref, vt_ref, u_ref, g_ref, win_ref):
    tm = x_ref.shape[0]
    halo = F32_SUBLANES
    tiles_per_seq = seq // tm
    pos = pl.program_id(0) % tiles_per_seq
    xe = jnp.concatenate([x_ref[...], xp_ref[...], xn_ref[...]], axis=0).astype(BF16)
    xb = xe[:tm]
    cw = cw_ref[...]
    cb = cb_ref[...]

    def plain(dst, w_ref, w_cols):
        def run():
            dst[...] = jnp.dot(xb, w_ref[:, w_cols], preferred_element_type=F32).astype(dst.dtype)
        return run

    def conv_cols(slot, c0):
        cols = slice(c0, c0 + INPROJ_COLS)
        win = win_ref.at[slot]
        pre = jnp.dot(xe, wqk_ref[:, cols], preferred_element_type=F32)
        win[0:halo, :] = jnp.where(pos > 0, pre[tm:tm + halo], 0.0)
        win[halo:halo + tm, :] = pre[:tm]
        win[halo + tm:, :] = jnp.where(pos < tiles_per_seq - 1, pre[tm + halo:], 0.0)
        for r0 in range(0, tm, INPROJ_STRIP):
            rows = slice(r0, r0 + INPROJ_STRIP)
            y = cb[:, cols]
            for k in range(M_CONV):
                first = r0 + halo + k - M_CONV // 2
                y = y + cw[k:k + 1, cols] * win[first:first + INPROJ_STRIP, :]
            y = y * _sigmoid(y)
            if c0 < M_WIDTH:
                qt_ref[cols, rows] = y.T.astype(BF16)
            else:
                u_ref[rows, U_K + c0 - M_WIDTH:U_K + c0 - M_WIDTH + INPROJ_COLS] = (
                    y * (M_HEAD_DIM ** -0.5)).astype(BF16)

    def v_cols(slot, c0):
        cols = slice(c0, c0 + INPROJ_COLS)
        win = win_ref.at[slot]
        win[0:tm, :] = jnp.dot(xb, wv_ref[:, cols], preferred_element_type=F32)
        for r0 in range(0, tm, INPROJ_STRIP):
            rows = slice(r0, r0 + INPROJ_STRIP)
            vt_ref[cols, rows] = win[rows, :].T.astype(BF16)

    half = (U_AQ - U_O) // 2
    plains = [plain(u_ref.at[:, U_O:U_O + half], wo_ref, slice(0, half)),
              plain(u_ref.at[:, U_O + half:U_AQ], wo_ref, slice(half, 2 * half)),
              plain(u_ref.at[:, U_AQ:U_AQ + half], waq_ref, slice(0, half)),
              plain(u_ref.at[:, U_AQ + half:U_AKV], waq_ref, slice(half, 2 * half)),
              plain(u_ref.at[:, U_AKV:], wakv_ref, slice(None)),
              plain(g_ref, wg_ref, slice(None))]
    passes = ([functools.partial(conv_cols, i % 2, c0)
               for i, c0 in enumerate(range(0, 2 * M_WIDTH, INPROJ_COLS))]
              + [functools.partial(v_cols, i % 2, c0)
                 for i, c0 in enumerate(range(0, M_WIDTH, INPROJ_COLS))])
    for run_pass, run_plain in zip(passes, plains):
        run_pass()
        run_plain()


def _inproj(layer, x2, seq, wmain, waq, wakv, wg, conv_w, conv_b):
    t, d = x2.shape
    tm = INPROJ_TM
    halo = F32_SUBLANES
    tail = [waq, wakv, wg, conv_w, conv_b]
    return pl.pallas_call(
        functools.partial(_inproj_kernel, seq),
        grid=(t // tm,),
        in_specs=[pl.BlockSpec((tm, d), lambda i: (i, 0)),
                  pl.BlockSpec((halo, d), lambda i: (jnp.maximum(i * (tm // halo) - 1, 0), 0)),
                  pl.BlockSpec((halo, d),
                               lambda i: (jnp.minimum((i + 1) * (tm // halo), t // halo - 1), 0)),
                  _layer_spec(wmain, layer, 2 * M_WIDTH, 0),
                  _layer_spec(wmain, layer, M_WIDTH, 2),
                  _layer_spec(wmain, layer, M_WIDTH, 3)]
                 + [_layer_spec(w, layer) for w in tail],
        out_specs=[pl.BlockSpec((M_WIDTH, tm), lambda i: (0, i)),
                   pl.BlockSpec((M_WIDTH, tm), lambda i: (0, i)),
                   pl.BlockSpec((tm, U_WIDTH), lambda i: (i, 0)),
                   pl.BlockSpec((tm, G_WIDTH), lambda i: (i, 0))],
        out_shape=[jax.ShapeDtypeStruct((M_WIDTH, t), BF16),
                   jax.ShapeDtypeStruct((M_WIDTH, t), BF16),
                   jax.ShapeDtypeStruct((t, U_WIDTH), BF16),
                   jax.ShapeDtypeStruct((t, G_WIDTH), F32)],
        scratch_shapes=[pltpu.VMEM((2, tm + 2 * halo, INPROJ_COLS), F32)],
        compiler_params=pltpu.CompilerParams(dimension_semantics=("arbitrary",),
                                             vmem_limit_bytes=VMEM_LIMIT),
        name="inproj",
    )(x2, x2, x2, wmain, wmain, wmain, *tail)


def _gateprep_kernel(g_ref, bias_ref, rows_ref, gt_ref, mp_ref):
    s = g_ref.shape[0]
    nc = s // CHUNK
    nd = 2 * M_HEADS

    def transpose_tile(i, carry):
        r0 = pl.multiple_of(i * LANES, LANES)
        tile = g_ref[pl.ds(r0, LANES), :] + bias_ref[...]
        gt_ref[:, pl.ds(r0, LANES)] = tile.T[:N_GATE, :]
        return carry

    lax.fori_loop(0, s // LANES, transpose_tile, 0)
    gt = gt_ref[...]
    ig = gt[0:nd]
    fg = gt[nd:2 * nd]
    lf = jnp.minimum(fg, 0.0) - jnp.log1p(jnp.exp(-jnp.abs(fg)))
    pos = lax.broadcasted_iota(jnp.int32, (nd, s), 1) & (CHUNK - 1)
    fwd_row = lax.broadcasted_iota(jnp.int32, (nd, s), 0) < M_HEADS

    def chunk_scans(x, op, identity):
        pre, suf = x, x
        sh = 1
        while sh < CHUNK:
            pre = op(pre, jnp.where(pos >= sh, pltpu.roll(pre, sh, 1), identity))
            suf = op(suf, jnp.where(pos < CHUNK - sh, pltpu.roll(suf, s - sh, 1), identity))
            sh *= 2
        return pre, suf

    lf_pre, lf_suf = chunk_scans(lf, jnp.add, 0.0)
    bcum = jnp.where(fwd_row, lf_pre, lf_suf)
    gtot = lf_pre + lf_suf - lf
    r = ig - bcum
    r_pre, r_suf = chunk_scans(r, jnp.maximum, -jnp.inf)
    rcummax = jnp.where(fwd_row, r_pre, r_suf)
    rmax = jnp.maximum(r_pre, r_suf)
    m = jnp.zeros((M_HEADS, CHUNK), F32)
    for c in range(nc):
        sl = slice(c * CHUNK, (c + 1) * CHUNK)
        mp_ref[0:M_HEADS, sl] = m
        m = gtot[0:M_HEADS, sl] + jnp.maximum(m, rmax[0:M_HEADS, sl])
    m = jnp.zeros((M_HEADS, CHUNK), F32)
    for c in reversed(range(nc)):
        sl = slice(c * CHUNK, (c + 1) * CHUNK)
        mp_ref[M_HEADS:nd, sl] = m
        m = gtot[M_HEADS:nd, sl] + jnp.maximum(m, rmax[M_HEADS:nd, sl])
    m_in = mp_ref[...]
    mm = jnp.maximum(m_in, rmax)
    mu = jnp.maximum(rcummax, m_in)
    table = (r, mu, bcum + mu, jnp.exp(m_in - mu), jnp.exp(r - mm), jnp.exp(m_in - mm))
    pad = jnp.zeros((GATE_ROWS - 2 * len(table), s), F32)
    for h in range(M_HEADS):
        pieces = []
        for arr in table:
            pieces += [arr[h:h + 1], arr[M_HEADS + h:M_HEADS + h + 1]]
        rows_ref[h] = jnp.concatenate(pieces + [pad], axis=0)


def _gateprep(layer, g2, bias, batch, seq):
    return pl.pallas_call(
        _gateprep_kernel,
        grid=(batch,),
        in_specs=[pl.BlockSpec((seq, G_WIDTH), lambda b: (b, 0)),
                  pl.BlockSpec((None, 1, G_WIDTH), lambda b: (layer, 0, 0))],
        out_specs=pl.BlockSpec((None, M_HEADS, GATE_ROWS, seq), lambda b: (b, 0, 0, 0)),
        out_shape=jax.ShapeDtypeStruct((batch, M_HEADS, GATE_ROWS, seq), F32),
        scratch_shapes=[pltpu.VMEM((N_GATE, seq), F32),
                        pltpu.VMEM((2 * M_HEADS, seq), F32)],
        compiler_params=pltpu.CompilerParams(dimension_semantics=("arbitrary",),
                                             vmem_limit_bytes=VMEM_LIMIT),
        name="gateprep",
    )(g2, bias)


def _mlstm_kernel(qt_ref, vin_ref, ks_ref, o_ref, rows_ref, ng_ref, out_ref,
                  vt_ref, csf_ref, csb_ref, cf_ref, cb_ref):
    seq = ks_ref.shape[0]
    nc = seq // CHUNK
    dh = M_HEAD_DIM
    de = dh + M_EXT

    vt_ref[0:dh, :] = vin_ref[...]
    vt_ref[dh:de, :] = (lax.broadcasted_iota(jnp.int32, (M_EXT, seq), 0) == 0).astype(BF16)

    def gate_row(kind, rev, r0):
        return rows_ref[2 * kind + rev:2 * kind + rev + 1, pl.ds(r0, CHUNK)]

    def state_step(ci, rev, c_ref, cs_ref):
        r0 = pl.multiple_of(ci * CHUNK, CHUNK)
        cs_ref[ci] = c_ref[...].astype(BF16)
        vw = (vt_ref[:, pl.ds(r0, CHUNK)].astype(F32) * gate_row(4, rev, r0)).astype(BF16)
        decay = gate_row(5, rev, r0)[:, 0:1]
        c_ref[...] = decay * c_ref[...] + jnp.dot(vw, ks_ref[pl.ds(r0, CHUNK), :],
                                                  preferred_element_type=F32)

    cf_ref[...] = jnp.zeros_like(cf_ref)
    cb_ref[...] = jnp.zeros_like(cb_ref)

    def state_body(c, carry):
        state_step(c, 0, cf_ref, csf_ref)
        state_step(nc - 1 - c, 1, cb_ref, csb_ref)
        return carry

    lax.fori_loop(0, nc, state_body, 0, unroll=4)

    s_idx = lax.broadcasted_iota(jnp.int32, (LANES, LANES), 0)
    t_idx = lax.broadcasted_iota(jnp.int32, (LANES, LANES), 1)
    tri = (s_idx <= t_idx, s_idx >= t_idx)
    krow = lax.broadcasted_iota(jnp.int32, (BF16_SUBLANES, CHUNK), 0)
    nblk = CHUNK // LANES

    def intra_weights(a, qk, rev):
        cols = []
        for bt in range(nblk):
            col = []
            for bs in range(nblk):
                blk = (slice(bs * LANES, (bs + 1) * LANES), slice(bt * LANES, (bt + 1) * LANES))
                if bs == bt:
                    w = qk[blk] * jnp.exp(jnp.where(tri[rev], a[blk], -jnp.inf))
                elif (bs > bt) == bool(rev):
                    w = qk[blk] * jnp.exp(a[blk])
                else:
                    w = jnp.zeros((LANES, LANES), F32)
                col.append(w.astype(BF16))
            cols.append(jnp.concatenate(col, axis=0))
        return jnp.concatenate(cols, axis=1)

    def split_rows(x, first):
        hi = x.astype(BF16).astype(F32)
        mid = (x - hi).astype(BF16).astype(F32)
        lo = x - hi - mid
        out = jnp.where(krow == first, hi, jnp.where(krow == first + 1, mid,
                                                     jnp.where(krow == first + 2, lo, 0.0)))
        ones = (krow >= 3 - first) & (krow < 6 - first)
        return jnp.where(ones, 1.0, out).astype(BF16)

    dirs = ((0, csf_ref), (1, csb_ref))

    def out_body(g, carry):
        cs = [g * M_OUT_GROUP + i for i in range(M_OUT_GROUP)]
        r0s = [pl.multiple_of(c * CHUNK, CHUNK) for c in cs]
        qts = [qt_ref[:, pl.ds(r0, CHUNK)] for r0 in r0s]
        qks = [jnp.dot(ks_ref[pl.ds(r0, CHUNK), :], qt, preferred_element_type=F32)
               for r0, qt in zip(r0s, qts)]
        a_s = [[lax.dot_general(split_rows(gate_row(0, rev, r0), 0),
                                split_rows(-gate_row(1, rev, r0), 3),
                                (((0,), (0,)), ((), ())), preferred_element_type=F32)
                for rev, _ in dirs] for r0 in r0s]
        inters = [[jnp.dot(cs_ref[c], qt, preferred_element_type=F32) for _, cs_ref in dirs]
                  for c, qt in zip(cs, qts)]
        ws = [[intra_weights(a[rev], qk, rev) for rev, _ in dirs] for a, qk in zip(a_s, qks)]
        tots = [[jnp.dot(vt_ref[:, pl.ds(r0, CHUNK)], w[rev], preferred_element_type=F32)
                 + gate_row(3, rev, r0) * inter[rev] for rev, _ in dirs]
                for r0, w, inter in zip(r0s, ws, inters)]
        for r0, tot in zip(r0s, tots):
            h = None
            for rev, _ in dirs:
                den = jnp.maximum(jnp.abs(tot[rev][dh:dh + 1, :]), jnp.exp(-gate_row(2, rev, r0)))
                hd = tot[rev][:dh, :] * (1.0 / den)
                h = hd if h is None else h + hd
            hc = h - jnp.mean(h, axis=0, keepdims=True)
            hn = hc * lax.rsqrt(jnp.mean(hc * hc, axis=0, keepdims=True) + MH_EPS)
            og = _sigmoid(o_ref[pl.ds(r0, CHUNK), :].astype(F32))
            out_ref[pl.ds(r0, CHUNK), :] = (hn.T * ng_ref[...] * og).astype(BF16)
        return carry

    lax.fori_loop(0, nc // M_OUT_GROUP, out_body, 0)


def _mlstm(layer, qt, vt, u, rows, norm_g, batch, seq):
    dh = M_HEAD_DIM
    de = dh + M_EXT
    nc = seq // CHUNK

    def tcol():
        return pl.BlockSpec((dh, seq), lambda b, h: (h, b))

    def ucol(base):
        return pl.BlockSpec((seq, dh), lambda b, h: (b, base // dh + h))

    return pl.pallas_call(
        _mlstm_kernel,
        grid=(batch, M_HEADS),
        in_specs=[tcol(), tcol(), ucol(U_K), ucol(U_O),
                  pl.BlockSpec((None, None, GATE_ROWS, seq), lambda b, h: (b, h, 0, 0)),
                  pl.BlockSpec((None, 1, dh), lambda b, h: (layer, 0, h))],
        out_specs=pl.BlockSpec((seq, dh), lambda b, h: (b, h)),
        out_shape=jax.ShapeDtypeStruct((batch * seq, M_WIDTH), BF16),
        scratch_shapes=[pltpu.VMEM((de, seq), BF16),
                        pltpu.VMEM((nc, de, dh), BF16),
                        pltpu.VMEM((nc, de, dh), BF16),
                        pltpu.VMEM((de, dh), F32),
                        pltpu.VMEM((de, dh), F32)],
        compiler_params=pltpu.CompilerParams(dimension_semantics=("arbitrary", "arbitrary"),
                                             vmem_limit_bytes=VMEM_LIMIT),
        name="mlstm",
    )(qt, vt, u, u, rows, norm_g)


A_DELTAS = (-A_BLOCK, 0, -2 * A_BLOCK)


def _attn_kernel(layer, sink_ref, q_ref, kv_ref, out_ref, bias_ref):
    seq = q_ref.shape[0]
    nb = seq // A_BLOCK
    span = 3 * A_BLOCK
    lo = lax.broadcasted_iota(jnp.int32, (A_BLOCK, LANES), 1) < A_HEAD_DIM
    lo_kv = lax.broadcasted_iota(jnp.int32, (span, LANES), 1) < A_HEAD_DIM
    sum_lane = lax.broadcasted_iota(jnp.int32, (span, LANES), 1)
    qi = lax.broadcasted_iota(jnp.int32, (A_BLOCK, span), 0)
    ki = lax.broadcasted_iota(jnp.int32, (A_BLOCK, span), 1)
    for variant, delta in enumerate(A_DELTAS):
        rel = jnp.abs(ki - qi + delta)
        base = jnp.where(rel <= WINDOW, -rel.astype(F32), -jnp.inf)
        for j in range(A_GROUP):
            for half in range(A_KV_HEADS):
                head = j + A_GROUP * half
                bias_ref[variant * A_GROUP + j, :, half * span:(half + 1) * span] = (
                    base * (2.0 ** -(head + 1)))

    def blocks(i, carry):
        q0s, kds, vds, variants = [], [], [], []
        for b in range(A_BLOCKS_PER_ITER):
            n = i * A_BLOCKS_PER_ITER + b
            q0 = pl.multiple_of(n * A_BLOCK, A_BLOCK)
            k0 = pl.multiple_of(jnp.clip(q0 - A_BLOCK, 0, seq - span), A_BLOCK)
            k3 = kv_ref[pl.ds(k0, span), 0:A_KV_WIDTH]
            v3 = kv_ref[pl.ds(k0, span), A_KV_WIDTH:2 * A_KV_WIDTH]
            zero = jnp.zeros_like(k3)
            q0s.append(q0)
            variants.append(jnp.where(n == 0, 1, jnp.where(n == nb - 1, 2, 0)))
            kds.append(jnp.concatenate([jnp.where(lo_kv, k3, zero), jnp.where(lo_kv, zero, k3)],
                                       axis=0))
            vds.append(jnp.concatenate(
                [jnp.concatenate([jnp.where(lo_kv, v3, zero), (sum_lane == 0).astype(BF16)], axis=1),
                 jnp.concatenate([jnp.where(lo_kv, zero, v3), (sum_lane == 1).astype(BF16)], axis=1)],
                axis=0))
        work = [(b, j) for b in range(A_BLOCKS_PER_ITER) for j in range(A_GROUP)]
        scs = [lax.dot_general(q_ref[pl.ds(q0s[b], A_BLOCK), j * LANES:(j + 1) * LANES], kds[b],
                               (((1,), (1,)), ((), ())), preferred_element_type=F32)
               + bias_ref[variants[b] * A_GROUP + j] for b, j in work]
        es, corrs = [], []
        for (b, j), sc in zip(work, scs):
            e, corr = [], []
            for half in range(A_KV_HEADS):
                sk = sink_ref[layer, j + A_GROUP * half]
                sch = sc[:, half * span:(half + 1) * span]
                mx = jnp.maximum(jnp.max(sch, axis=1, keepdims=True), sk)
                e.append(jnp.exp(sch - mx).astype(BF16))
                corr.append(jnp.exp(sk - mx))
            es.append(jnp.concatenate(e, axis=1))
            corrs.append(corr)
        pvs = [jnp.dot(e, vds[b], preferred_element_type=F32) for (b, j), e in zip(work, es)]
        for (b, j), pv, corr in zip(work, pvs, corrs):
            rden = [1.0 / (pv[:, LANES + half:LANES + half + 1] + corr[half])
                    for half in range(A_KV_HEADS)]
            out_ref[pl.ds(q0s[b], A_BLOCK), j * LANES:(j + 1) * LANES] = (
                pv[:, :LANES] * jnp.where(lo, rden[0], rden[1])).astype(BF16)
        return carry

    lax.fori_loop(0, nb // A_BLOCKS_PER_ITER, blocks, 0)


def _attn(layer, u, sink, batch, seq):
    return pl.pallas_call(
        functools.partial(_attn_kernel, layer),
        grid=(batch,),
        in_specs=[pl.BlockSpec(memory_space=pltpu.SMEM),
                  pl.BlockSpec((seq, A_WIDTH), lambda b: (b, U_AQ // A_WIDTH)),
                  pl.BlockSpec((seq, 2 * A_KV_WIDTH), lambda b: (b, U_AKV // (2 * A_KV_WIDTH)))],
        out_specs=pl.BlockSpec((seq, A_WIDTH), lambda b: (b, 0)),
        out_shape=jax.ShapeDtypeStruct((batch * seq, A_WIDTH), BF16),
        scratch_shapes=[pltpu.VMEM((len(A_DELTAS) * A_GROUP, A_BLOCK, 6 * A_BLOCK), F32)],
        compiler_params=pltpu.CompilerParams(dimension_semantics=("arbitrary",),
                                             vmem_limit_bytes=VMEM_LIMIT),
        name="attn",
    )(sink, u, u)


def _ff_chunks(d_ff):
    step = 1024
    return [(c0, min(c0 + step, d_ff)) for c0 in range(0, d_ff, step)]


def _post_kernel(alpha, x_ref, hm_ref, ha_ref, p_ref, wom_ref, woa_ref, wg_ref, wu_ref, wd_ref,
                 wpg_ref, wpp_ref, ln1g_ref, ln1b_ref, ln2g_ref, ln2b_ref, y_ref):
    subs = [slice(r0, r0 + POST_SUB) for r0 in range(0, x_ref.shape[0], POST_SUB)]
    x1s = []
    for rows in subs:
        mix = jnp.dot(hm_ref[rows, :], wom_ref[...], preferred_element_type=F32)
        mix = mix + jnp.dot(ha_ref[rows, :], woa_ref[...], preferred_element_type=F32)
        x1s.append(_layer_norm(alpha * x_ref[rows, :] + mix, ln1g_ref[...], ln1b_ref[...]))
    accs = []
    for rows, x1 in zip(subs, x1s):
        x1b = x1.astype(BF16)
        ple_gate = _sigmoid(jnp.dot(x1b, wpg_ref[...], preferred_element_type=F32))
        acc = alpha * x1 + ple_gate * jnp.dot(p_ref[rows, :].astype(BF16), wpp_ref[...],
                                              preferred_element_type=F32)
        for c0, c1 in _ff_chunks(wd_ref.shape[0]):
            gate = jnp.dot(x1b, wg_ref[:, c0:c1], preferred_element_type=F32)
            up = jnp.dot(x1b, wu_ref[:, c0:c1], preferred_element_type=F32)
            hid = (gate * _sigmoid(gate) * up).astype(BF16)
            acc = acc + jnp.dot(hid, wd_ref[c0:c1, :], preferred_element_type=F32)
        accs.append(acc)
    for rows, acc in zip(subs, accs):
        y_ref[rows, :] = _layer_norm(acc, ln2g_ref[...], ln2b_ref[...])


def _post(layer, alpha, x2, hm, ha, p3, wom, woa, wgu, wd, wpg, wpp, ln1g, ln1b, ln2g, ln2b):
    t, d = x2.shape
    tm = POST_TM
    d_ff = wd.shape[1]

    def rows(width):
        return pl.BlockSpec((tm, width), lambda i: (i, 0))

    tail = [wd, wpg, wpp, ln1g, ln1b, ln2g, ln2b]
    return pl.pallas_call(
        functools.partial(_post_kernel, alpha),
        grid=(t // tm,),
        in_specs=[rows(d), rows(hm.shape[1]), rows(ha.shape[1]),
                  pl.BlockSpec((None, tm, p3.shape[2]), lambda i: (layer, i, 0)),
                  _layer_spec(wom, layer), _layer_spec(woa, layer),
                  _layer_spec(wgu, layer, d_ff, 0), _layer_spec(wgu, layer, d_ff, 1)]
                 + [_layer_spec(w, layer) for w in tail],
        out_specs=rows(d),
        out_shape=jax.ShapeDtypeStruct((t, d), F32),
        compiler_params=pltpu.CompilerParams(dimension_semantics=("arbitrary",),
                                             vmem_limit_bytes=VMEM_LIMIT),
        name="post",
    )(x2, hm, ha, p3, wom, woa, wgu, wgu, *tail)


def _pair_heads(w, axis):
    shape = w.shape
    w = w.reshape(shape[:axis] + (A_KV_HEADS, A_GROUP, A_HEAD_DIM) + shape[axis + 1:])
    return jnp.swapaxes(w, axis, axis + 1).reshape(shape)


def kernel(x, p, w_in, b_gate, conv_w, conv_b, mlstm_norm_g, attn_sink, w_out, ln1_g, ln1_b,
           w_ffn_in, w_ffn_out, ln2_g, ln2_b, w_ple_gate, w_ple_proj):
    batch, seq, d = x.shape
    depth = w_in.shape[0]
    t = batch * seq
    alpha = float((2 * depth) ** 0.25)
    assert seq % (CHUNK * M_OUT_GROUP) == 0 and seq >= 3 * A_BLOCK
    assert seq % (A_BLOCK * A_BLOCKS_PER_ITER) == 0
    assert t % INPROJ_TM == 0 and t % POST_TM == 0
    assert seq % INPROJ_TM == 0
    assert w_in.shape[2] == 4 * M_WIDTH + N_GATE + A_WIDTH + 2 * A_KV_WIDTH

    g0 = 4 * M_WIDTH
    aq0 = g0 + N_GATE
    akv0 = aq0 + A_WIDTH
    wmain = w_in[:, :, :g0].astype(BF16)
    waq = (_pair_heads(w_in[:, :, aq0:akv0], 2) * (A_HEAD_DIM ** -0.5)).astype(BF16)
    wakv = w_in[:, :, akv0:].astype(BF16)
    wgate = jnp.pad(w_in[:, :, g0:aq0], ((0, 0), (0, 0), (0, G_WIDTH - N_GATE))).astype(BF16)
    gate_bias = jnp.pad(b_gate, ((0, 0), (0, G_WIDTH - N_GATE)))[:, None, :]
    wom = w_out[:, :M_WIDTH].astype(BF16)
    woa = _pair_heads(w_out[:, M_WIDTH:], 1).astype(BF16)
    wgu = w_ffn_in.astype(BF16)
    wd = w_ffn_out.astype(BF16)
    wpg = w_ple_gate.astype(BF16)
    wpp = w_ple_proj.astype(BF16)
    conv_b3, norm_g3, ln1_g3, ln1_b3, ln2_g3, ln2_b3 = (
        a[:, None, :] for a in (conv_b, mlstm_norm_g, ln1_g, ln1_b, ln2_g, ln2_b))

    h = x.reshape(t, d)
    p3 = p.reshape(depth, t, p.shape[-1])
    for i in range(depth):
        qt, vt, u, g = _inproj(i, h, seq, wmain, waq, wakv, wgate, conv_w, conv_b3)
        rows = _gateprep(i, g, gate_bias, batch, seq)
        hm = _mlstm(i, qt, vt, u, rows, norm_g3, batch, seq)
        ha = _attn(i, u, attn_sink, batch, seq)
        h = _post(i, alpha, h, hm, ha, p3, wom, woa, wgu, wd, wpg, wpp,
                  ln1_g3, ln1_b3, ln2_g3, ln2_b3)
    return h.reshape(batch, seq, d)
```

```python
import functools

import jax
import jax.numpy as jnp
from jax import lax
from jax.experimental import pallas as pl
from jax.experimental.pallas import tpu as pltpu

F32 = jnp.float32
BF16 = jnp.bfloat16

M_HEADS = 4
M_HEAD_DIM = 128
M_WIDTH = M_HEADS * M_HEAD_DIM
M_CONV = 5
N_GATE = 4 * M_HEADS
A_HEADS = 8
A_KV_HEADS = 2
A_HEAD_DIM = 64
A_GROUP = A_HEADS // A_KV_HEADS
A_WIDTH = A_HEADS * A_HEAD_DIM
A_KV_WIDTH = A_KV_HEADS * A_HEAD_DIM
WINDOW = 128
LN_EPS = 1e-5
MH_EPS = 1e-6

LANES = 128
F32_SUBLANES = 8
BF16_SUBLANES = 16

CHUNK = 256
A_BLOCK = 128
A_BLOCKS_PER_ITER = 4
U_K, U_O = 0, M_WIDTH
U_AQ = 2 * M_WIDTH
U_AKV = U_AQ + A_WIDTH
U_WIDTH = U_AKV + 2 * A_KV_WIDTH
G_WIDTH = LANES
INPROJ_COLS = 256
INPROJ_STRIP = 128
GATE_ROWS = 16
M_OUT_GROUP = 4
M_EXT = BF16_SUBLANES
INPROJ_TM = 1024
POST_TM = 512
POST_SUB = 256
VMEM_LIMIT = 56 * 1024 * 1024


def _sigmoid(x):
    return 1.0 / (1.0 + jnp.exp(-x))


def _layer_norm(z, g, b):
    mu = jnp.mean(z, axis=-1, keepdims=True)
    zc = z - mu
    var = jnp.mean(zc * zc, axis=-1, keepdims=True)
    return zc * lax.rsqrt(var + LN_EPS) * g + b


def _layer_spec(stacked, layer, cols=None, col_block=0):
    _, rows, all_cols = stacked.shape
    return pl.BlockSpec((None, rows, cols or all_cols), lambda *_: (layer, 0, col_block),
                        pipeline_mode=pl.Buffered(1))


def _inproj_kernel(seq, x_ref, xp_ref, xn_ref, wqk_ref, wv_ref, wo_ref, waq_ref, wakv_ref, wg_ref,
                   gb_ref, cw_ref, cb_ref, qt_ref, vt_ref, u_ref, gt_ref, win_ref):
    tm = x_ref.shape[0]
    halo = F32_SUBLANES
    tiles_per_seq = seq // tm
    pos = pl.program_id(0) % tiles_per_seq
    xe = jnp.concatenate([x_ref[...], xp_ref[...], xn_ref[...]], axis=0).astype(BF16)
    xb = xe[:tm]
    cw = cw_ref[...]
    cb = cb_ref[...]

    def plain(dst, w_ref, w_cols):
        def run():
            dst[...] = jnp.dot(xb, w_ref[:, w_cols], preferred_element_type=F32).astype(dst.dtype)
        return run

    def conv_cols(slot, c0):
        cols = slice(c0, c0 + INPROJ_COLS)
        win = win_ref.at[slot]
        pre = jnp.dot(xe, wqk_ref[:, cols], preferred_element_type=F32)
        win[0:halo, :] = jnp.where(pos > 0, pre[tm:tm + halo], 0.0)
        win[halo:halo + tm, :] = pre[:tm]
        win[halo + tm:, :] = jnp.where(pos < tiles_per_seq - 1, pre[tm + halo:], 0.0)
        for r0 in range(0, tm, INPROJ_STRIP):
            rows = slice(r0, r0 + INPROJ_STRIP)
            y = cb[:, cols]
            for k in range(M_CONV):
                first = r0 + halo + k - M_CONV // 2
                y = y + cw[k:k + 1, cols] * win[first:first + INPROJ_STRIP, :]
            y = y * _sigmoid(y)
            if c0 < M_WIDTH:
                qt_ref[cols, rows] = y.T.astype(BF16)
            else:
                u_ref[rows, U_K + c0 - M_WIDTH:U_K + c0 - M_WIDTH + INPROJ_COLS] = (
                    y * (M_HEAD_DIM ** -0.5)).astype(BF16)

    def v_cols(slot, c0):
        cols = slice(c0, c0 + INPROJ_COLS)
        win = win_ref.at[slot]
        win[0:tm, :] = jnp.dot(xb, wv_ref[:, cols], preferred_element_type=F32)
        for r0 in range(0, tm, INPROJ_STRIP):
            rows = slice(r0, r0 + INPROJ_STRIP)
            vt_ref[cols, rows] = win[rows, :].T.astype(BF16)

    def gates():
        g = jnp.dot(xb, wg_ref[...], preferred_element_type=F32) + gb_ref[...]
        for r0 in range(0, tm, INPROJ_STRIP):
            rows = slice(r0, r0 + INPROJ_STRIP)
            gt_ref[:, rows] = g[rows].T[:N_GATE]

    half = (U_AQ - U_O) // 2
    plains = [plain(u_ref.at[:, U_O:U_O + half], wo_ref, slice(0, half)),
              plain(u_ref.at[:, U_O + half:U_AQ], wo_ref, slice(half, 2 * half)),
              plain(u_ref.at[:, U_AQ:U_AQ + half], waq_ref, slice(0, half)),
              plain(u_ref.at[:, U_AQ + half:U_AKV], waq_ref, slice(half, 2 * half)),
              plain(u_ref.at[:, U_AKV:], wakv_ref, slice(None)),
              gates]
    passes = ([functools.partial(conv_cols, i % 2, c0)
               for i, c0 in enumerate(range(0, 2 * M_WIDTH, INPROJ_COLS))]
              + [functools.partial(v_cols, i % 2, c0)
                 for i, c0 in enumerate(range(0, M_WIDTH, INPROJ_COLS))])
    for run_pass, run_plain in zip(passes, plains):
        run_pass()
        run_plain()


def _inproj(layer, x2, seq, wmain, waq, wakv, wg, gate_bias, conv_w, conv_b):
    t, d = x2.shape
    tm = INPROJ_TM
    halo = F32_SUBLANES
    tail = [waq, wakv, wg, gate_bias, conv_w, conv_b]
    return pl.pallas_call(
        functools.partial(_inproj_kernel, seq),
        grid=(t // tm,),
        in_specs=[pl.BlockSpec((tm, d), lambda i: (i, 0)),
                  pl.BlockSpec((halo, d), lambda i: (jnp.maximum(i * (tm // halo) - 1, 0), 0)),
                  pl.BlockSpec((halo, d),
                               lambda i: (jnp.minimum((i + 1) * (tm // halo), t // halo - 1), 0)),
                  _layer_spec(wmain, layer, 2 * M_WIDTH, 0),
                  _layer_spec(wmain, layer, M_WIDTH, 2),
                  _layer_spec(wmain, layer, M_WIDTH, 3)]
                 + [_layer_spec(w, layer) for w in tail],
        out_specs=[pl.BlockSpec((M_WIDTH, tm), lambda i: (0, i)),
                   pl.BlockSpec((M_WIDTH, tm), lambda i: (0, i)),
                   pl.BlockSpec((tm, U_WIDTH), lambda i: (i, 0)),
                   pl.BlockSpec((N_GATE, tm), lambda i: (0, i))],
        out_shape=[jax.ShapeDtypeStruct((M_WIDTH, t), BF16),
                   jax.ShapeDtypeStruct((M_WIDTH, t), BF16),
                   jax.ShapeDtypeStruct((t, U_WIDTH), BF16),
                   jax.ShapeDtypeStruct((N_GATE, t), F32)],
        scratch_shapes=[pltpu.VMEM((2, tm + 2 * halo, INPROJ_COLS), F32)],
        compiler_params=pltpu.CompilerParams(dimension_semantics=("arbitrary",),
                                             vmem_limit_bytes=VMEM_LIMIT),
        name="inproj",
    )(x2, x2, x2, wmain, wmain, wmain, *tail)


def _gateprep_kernel(gt_ref, rows_ref, mp_ref):
    s = gt_ref.shape[1]
    nc = s // CHUNK
    nd = 2 * M_HEADS
    gt = gt_ref[...]
    ig = gt[0:nd]
    fg = gt[nd:2 * nd]
    lf = jnp.minimum(fg, 0.0) - jnp.log1p(jnp.exp(-jnp.abs(fg)))
    pos = lax.broadcasted_iota(jnp.int32, (nd, s), 1) & (CHUNK - 1)
    fwd_row = lax.broadcasted_iota(jnp.int32, (nd, s), 0) < M_HEADS

    def chunk_scans(x, op, identity):
        pre, suf = x, x
        sh = 1
        while sh < CHUNK:
            pre = op(pre, jnp.where(pos >= sh, pltpu.roll(pre, sh, 1), identity))
            suf = op(suf, jnp.where(pos < CHUNK - sh, pltpu.roll(suf, s - sh, 1), identity))
            sh *= 2
        return pre, suf

    lf_pre, lf_suf = chunk_scans(lf, jnp.add, 0.0)
    bcum = jnp.where(fwd_row, lf_pre, lf_suf)
    gtot = lf_pre + lf_suf - lf
    r = ig - bcum
    r_pre, r_suf = chunk_scans(r, jnp.maximum, -jnp.inf)
    rcummax = jnp.where(fwd_row, r_pre, r_suf)
    rmax = jnp.maximum(r_pre, r_suf)
    m = jnp.zeros((M_HEADS, CHUNK), F32)
    for c in range(nc):
        sl = slice(c * CHUNK, (c + 1) * CHUNK)
        mp_ref[0:M_HEADS, sl] = m
        m = gtot[0:M_HEADS, sl] + jnp.maximum(m, rmax[0:M_HEADS, sl])
    m = jnp.zeros((M_HEADS, CHUNK), F32)
    for c in reversed(range(nc)):
        sl = slice(c * CHUNK, (c + 1) * CHUNK)
        mp_ref[M_HEADS:nd, sl] = m
        m = gtot[M_HEADS:nd, sl] + jnp.maximum(m, rmax[M_HEADS:nd, sl])
    m_in = mp_ref[...]
    mm = jnp.maximum(m_in, rmax)
    mu = jnp.maximum(rcummax, m_in)
    table = (r, mu, bcum + mu, jnp.exp(m_in - mu), jnp.exp(r - mm), jnp.exp(m_in - mm))
    pad = jnp.zeros((GATE_ROWS - 2 * len(table), s), F32)
    for h in range(M_HEADS):
        pieces = []
        for arr in table:
            pieces += [arr[h:h + 1], arr[M_HEADS + h:M_HEADS + h + 1]]
        rows_ref[h] = jnp.concatenate(pieces + [pad], axis=0)


def _gateprep(gt, batch, seq):
    return pl.pallas_call(
        _gateprep_kernel,
        grid=(batch,),
        in_specs=[pl.BlockSpec((N_GATE, seq), lambda b: (0, b))],
        out_specs=pl.BlockSpec((None, M_HEADS, GATE_ROWS, seq), lambda b: (b, 0, 0, 0)),
        out_shape=jax.ShapeDtypeStruct((batch, M_HEADS, GATE_ROWS, seq), F32),
        scratch_shapes=[pltpu.VMEM((2 * M_HEADS, seq), F32)],
        compiler_params=pltpu.CompilerParams(dimension_semantics=("arbitrary",),
                                             vmem_limit_bytes=VMEM_LIMIT),
        name="gateprep",
    )(gt)


def _mlstm_kernel(qt_ref, vin_ref, ks_ref, o_ref, rows_ref, ng_ref, out_ref,
                  vt_ref, csf_ref, csb_ref, cf_ref, cb_ref):
    seq = ks_ref.shape[0]
    nc = seq // CHUNK
    dh = M_HEAD_DIM
    de = dh + M_EXT

    vt_ref[0:dh, :] = vin_ref[...]
    vt_ref[dh:de, :] = (lax.broadcasted_iota(jnp.int32, (M_EXT, seq), 0) == 0).astype(BF16)

    def gate_row(kind, rev, r0):
        return rows_ref[2 * kind + rev:2 * kind + rev + 1, pl.ds(r0, CHUNK)]

    def state_step(ci, rev, c_ref, cs_ref):
        r0 = pl.multiple_of(ci * CHUNK, CHUNK)
        cs_ref[ci] = c_ref[...].astype(BF16)
        vw = (vt_ref[:, pl.ds(r0, CHUNK)].astype(F32) * gate_row(4, rev, r0)).astype(BF16)
        decay = gate_row(5, rev, r0)[:, 0:1]
        c_ref[...] = decay * c_ref[...] + jnp.dot(vw, ks_ref[pl.ds(r0, CHUNK), :],
                                                  preferred_element_type=F32)

    cf_ref[...] = jnp.zeros_like(cf_ref)
    cb_ref[...] = jnp.zeros_like(cb_ref)

    def state_body(c, carry):
        state_step(c, 0, cf_ref, csf_ref)
        state_step(nc - 1 - c, 1, cb_ref, csb_ref)
        return carry

    lax.fori_loop(0, nc, state_body, 0, unroll=4)

    s_idx = lax.broadcasted_iota(jnp.int32, (LANES, LANES), 0)
    t_idx = lax.broadcasted_iota(jnp.int32, (LANES, LANES), 1)
    tri = (s_idx <= t_idx, s_idx >= t_idx)
    krow = lax.broadcasted_iota(jnp.int32, (BF16_SUBLANES, CHUNK), 0)
    nblk = CHUNK // LANES

    def intra_weights(a, qk, rev):
        cols = []
        for bt in range(nblk):
            col = []
            for bs in range(nblk):
                blk = (slice(bs * LANES, (bs + 1) * LANES), slice(bt * LANES, (bt + 1) * LANES))
                if bs == bt:
                    w = qk[blk] * jnp.exp(jnp.where(tri[rev], a[blk], -jnp.inf))
                elif (bs > bt) == bool(rev):
                    w = qk[blk] * jnp.exp(a[blk])
                else:
                    w = jnp.zeros((LANES, LANES), F32)
                col.append(w.astype(BF16))
            cols.append(jnp.concatenate(col, axis=0))
        return jnp.concatenate(cols, axis=1)

    def split_rows(x, first):
        hi = x.astype(BF16).astype(F32)
        mid = (x - hi).astype(BF16).astype(F32)
        lo = x - hi - mid
        out = jnp.where(krow == first, hi, jnp.where(krow == first + 1, mid,
                                                     jnp.where(krow == first + 2, lo, 0.0)))
        ones = (krow >= 3 - first) & (krow < 6 - first)
        return jnp.where(ones, 1.0, out).astype(BF16)

    dirs = ((0, csf_ref), (1, csb_ref))

    def out_body(g, carry):
        cs = [g * M_OUT_GROUP + i for i in range(M_OUT_GROUP)]
        r0s = [pl.multiple_of(c * CHUNK, CHUNK) for c in cs]
        qts = [qt_ref[:, pl.ds(r0, CHUNK)] for r0 in r0s]
        qks = [jnp.dot(ks_ref[pl.ds(r0, CHUNK), :], qt, preferred_element_type=F32)
               for r0, qt in zip(r0s, qts)]
        a_s = [[lax.dot_general(split_rows(gate_row(0, rev, r0), 0),
                                split_rows(-gate_row(1, rev, r0), 3),
                                (((0,), (0,)), ((), ())), preferred_element_type=F32)
                for rev, _ in dirs] for r0 in r0s]
        inters = [[jnp.dot(cs_ref[c], qt, preferred_element_type=F32) for _, cs_ref in dirs]
                  for c, qt in zip(cs, qts)]
        ws = [[intra_weights(a[rev], qk, rev) for rev, _ in dirs] for a, qk in zip(a_s, qks)]
        tots = [[jnp.dot(vt_ref[:, pl.ds(r0, CHUNK)], w[rev], preferred_element_type=F32)
                 + gate_row(3, rev, r0) * inter[rev] for rev, _ in dirs]
                for r0, w, inter in zip(r0s, ws, inters)]
        for r0, tot in zip(r0s, tots):
            h = None
            for rev, _ in dirs:
                den = jnp.maximum(jnp.abs(tot[rev][dh:dh + 1, :]), jnp.exp(-gate_row(2, rev, r0)))
                hd = tot[rev][:dh, :] * (1.0 / den)
                h = hd if h is None else h + hd
            hc = h - jnp.mean(h, axis=0, keepdims=True)
            hn = hc * lax.rsqrt(jnp.mean(hc * hc, axis=0, keepdims=True) + MH_EPS)
            og = _sigmoid(o_ref[pl.ds(r0, CHUNK), :].astype(F32))
            out_ref[pl.ds(r0, CHUNK), :] = (hn.T * ng_ref[...] * og).astype(BF16)
        return carry

    lax.fori_loop(0, nc // M_OUT_GROUP, out_body, 0)


def _mlstm(layer, qt, vt, u, rows, norm_g, batch, seq):
    dh = M_HEAD_DIM
    de = dh + M_EXT
    nc = seq // CHUNK

    def tcol():
        return pl.BlockSpec((dh, seq), lambda b, h: (h, b))

    def ucol(base):
        return pl.BlockSpec((seq, dh), lambda b, h: (b, base // dh + h))

    return pl.pallas_call(
        _mlstm_kernel,
        grid=(batch, M_HEADS),
        in_specs=[tcol(), tcol(), ucol(U_K), ucol(U_O),
                  pl.BlockSpec((None, None, GATE_ROWS, seq), lambda b, h: (b, h, 0, 0)),
                  pl.BlockSpec((None, 1, dh), lambda b, h: (layer, 0, h))],
        out_specs=pl.BlockSpec((seq, dh), lambda b, h: (b, h)),
        out_shape=jax.ShapeDtypeStruct((batch * seq, M_WIDTH), BF16),
        scratch_shapes=[pltpu.VMEM((de, seq), BF16),
                        pltpu.VMEM((nc, de, dh), BF16),
                        pltpu.VMEM((nc, de, dh), BF16),
                        pltpu.VMEM((de, dh), F32),
                        pltpu.VMEM((de, dh), F32)],
        compiler_params=pltpu.CompilerParams(dimension_semantics=("arbitrary", "arbitrary"),
                                             vmem_limit_bytes=VMEM_LIMIT),
        name="mlstm",
    )(qt, vt, u, u, rows, norm_g)


A_DELTAS = (-A_BLOCK, 0, -2 * A_BLOCK)


def _attn_kernel(layer, sink_ref, q_ref, kv_ref, out_ref, bias_ref):
    seq = q_ref.shape[0]
    nb = seq // A_BLOCK
    span = 3 * A_BLOCK
    lo = lax.broadcasted_iota(jnp.int32, (A_BLOCK, LANES), 1) < A_HEAD_DIM
    lo_kv = lax.broadcasted_iota(jnp.int32, (span, LANES), 1) < A_HEAD_DIM
    sum_lane = lax.broadcasted_iota(jnp.int32, (span, LANES), 1)
    qi = lax.broadcasted_iota(jnp.int32, (A_BLOCK, span), 0)
    ki = lax.broadcasted_iota(jnp.int32, (A_BLOCK, span), 1)
    for variant, delta in enumerate(A_DELTAS):
        rel = jnp.abs(ki - qi + delta)
        base = jnp.where(rel <= WINDOW, -rel.astype(F32), -jnp.inf)
        for j in range(A_GROUP):
            for half in range(A_KV_HEADS):
                head = j + A_GROUP * half
                bias_ref[variant * A_GROUP + j, :, half * span:(half + 1) * span] = (
                    base * (2.0 ** -(head + 1)))

    def blocks(i, carry):
        q0s, kds, vds, variants = [], [], [], []
        for b in range(A_BLOCKS_PER_ITER):
            n = i * A_BLOCKS_PER_ITER + b
            q0 = pl.multiple_of(n * A_BLOCK, A_BLOCK)
            k0 = pl.multiple_of(jnp.clip(q0 - A_BLOCK, 0, seq - span), A_BLOCK)
            k3 = kv_ref[pl.ds(k0, span), 0:A_KV_WIDTH]
            v3 = kv_ref[pl.ds(k0, span), A_KV_WIDTH:2 * A_KV_WIDTH]
            zero = jnp.zeros_like(k3)
            q0s.append(q0)
            variants.append(jnp.where(n == 0, 1, jnp.where(n == nb - 1, 2, 0)))
            kds.append(jnp.concatenate([jnp.where(lo_kv, k3, zero), jnp.where(lo_kv, zero, k3)],
                                       axis=0))
            vds.append(jnp.concatenate(
                [jnp.concatenate([jnp.where(lo_kv, v3, zero), (sum_lane == 0).astype(BF16)], axis=1),
                 jnp.concatenate([jnp.where(lo_kv, zero, v3), (sum_lane == 1).astype(BF16)], axis=1)],
                axis=0))
        work = [(b, j) for b in range(A_BLOCKS_PER_ITER) for j in range(A_GROUP)]
        scs = [lax.dot_general(q_ref[pl.ds(q0s[b], A_BLOCK), j * LANES:(j + 1) * LANES], kds[b],
                               (((1,), (1,)), ((), ())), preferred_element_type=F32)
               + bias_ref[variants[b] * A_GROUP + j] for b, j in work]
        es, corrs = [], []
        for (b, j), sc in zip(work, scs):
            e, corr = [], []
            for half in range(A_KV_HEADS):
                sk = sink_ref[layer, j + A_GROUP * half]
                sch = sc[:, half * span:(half + 1) * span]
                mx = jnp.maximum(jnp.max(sch, axis=1, keepdims=True), sk)
                e.append(jnp.exp(sch - mx).astype(BF16))
                corr.append(jnp.exp(sk - mx))
            es.append(jnp.concatenate(e, axis=1))
            corrs.append(corr)
        pvs = [jnp.dot(e, vds[b], preferred_element_type=F32) for (b, j), e in zip(work, es)]
        for (b, j), pv, corr in zip(work, pvs, corrs):
            rden = [1.0 / (pv[:, LANES + half:LANES + half + 1] + corr[half])
                    for half in range(A_KV_HEADS)]
            out_ref[pl.ds(q0s[b], A_BLOCK), j * LANES:(j + 1) * LANES] = (
                pv[:, :LANES] * jnp.where(lo, rden[0], rden[1])).astype(BF16)
        return carry

    lax.fori_loop(0, nb // A_BLOCKS_PER_ITER, blocks, 0)


def _attn(layer, u, sink, batch, seq):
    return pl.pallas_call(
        functools.partial(_attn_kernel, layer),
        grid=(batch,),
        in_specs=[pl.BlockSpec(memory_space=pltpu.SMEM),
                  pl.BlockSpec((seq, A_WIDTH), lambda b: (b, U_AQ // A_WIDTH)),
                  pl.BlockSpec((seq, 2 * A_KV_WIDTH), lambda b: (b, U_AKV // (2 * A_KV_WIDTH)))],
        out_specs=pl.BlockSpec((seq, A_WIDTH), lambda b: (b, 0)),
        out_shape=jax.ShapeDtypeStruct((batch * seq, A_WIDTH), BF16),
        scratch_shapes=[pltpu.VMEM((len(A_DELTAS) * A_GROUP, A_BLOCK, 6 * A_BLOCK), F32)],
        compiler_params=pltpu.CompilerParams(dimension_semantics=("arbitrary",),
                                             vmem_limit_bytes=VMEM_LIMIT),
        name="attn",
    )(sink, u, u)


def _ff_chunks(d_ff):
    step = 1024
    return [(c0, min(c0 + step, d_ff)) for c0 in range(0, d_ff, step)]


def _post_kernel(alpha, x_ref, hm_ref, ha_ref, p_ref, wom_ref, woa_ref, wg_ref, wu_ref, wd_ref,
                 wpg_ref, wpp_ref, ln1g_ref, ln1b_ref, ln2g_ref, ln2b_ref, y_ref):
    subs = [slice(r0, r0 + POST_SUB) for r0 in range(0, x_ref.shape[0], POST_SUB)]
    x1s = []
    for rows in subs:
        mix = jnp.dot(hm_ref[rows, :], wom_ref[...], preferred_element_type=F32)
        mix = mix + jnp.dot(ha_ref[rows, :], woa_ref[...], preferred_element_type=F32)
        x1s.append(_layer_norm(alpha * x_ref[rows, :] + mix, ln1g_ref[...], ln1b_ref[...]))
    accs = []
    for rows, x1 in zip(subs, x1s):
        x1b = x1.astype(BF16)
        ple_gate = _sigmoid(jnp.dot(x1b, wpg_ref[...], preferred_element_type=F32))
        acc = alpha * x1 + ple_gate * jnp.dot(p_ref[rows, :].astype(BF16), wpp_ref[...],
                                              preferred_element_type=F32)
        for c0, c1 in _ff_chunks(wd_ref.shape[0]):
            gate = jnp.dot(x1b, wg_ref[:, c0:c1], preferred_element_type=F32)
            up = jnp.dot(x1b, wu_ref[:, c0:c1], preferred_element_type=F32)
            hid = (gate * _sigmoid(gate) * up).astype(BF16)
            acc = acc + jnp.dot(hid, wd_ref[c0:c1, :], preferred_element_type=F32)
        accs.append(acc)
    for rows, acc in zip(subs, accs):
        y_ref[rows, :] = _layer_norm(acc, ln2g_ref[...], ln2b_ref[...])


def _post(layer, alpha, x2, hm, ha, p3, wom, woa, wgu, wd, wpg, wpp, ln1g, ln1b, ln2g, ln2b):
    t, d = x2.shape
    tm = POST_TM
    d_ff = wd.shape[1]

    def rows(width):
        return pl.BlockSpec((tm, width), lambda i: (i, 0))

    tail = [wd, wpg, wpp, ln1g, ln1b, ln2g, ln2b]
    return pl.pallas_call(
        functools.partial(_post_kernel, alpha),
        grid=(t // tm,),
        in_specs=[rows(d), rows(hm.shape[1]), rows(ha.shape[1]),
                  pl.BlockSpec((None, tm, p3.shape[2]), lambda i: (layer, i, 0)),
                  _layer_spec(wom, layer), _layer_spec(woa, layer),
                  _layer_spec(wgu, layer, d_ff, 0), _layer_spec(wgu, layer, d_ff, 1)]
                 + [_layer_spec(w, layer) for w in tail],
        out_specs=rows(d),
        out_shape=jax.ShapeDtypeStruct((t, d), F32),
        compiler_params=pltpu.CompilerParams(dimension_semantics=("arbitrary",),
                                             vmem_limit_bytes=VMEM_LIMIT),
        name="post",
    )(x2, hm, ha, p3, wom, woa, wgu, wgu, *tail)


def _pair_heads(w, axis):
    shape = w.shape
    w = w.reshape(shape[:axis] + (A_KV_HEADS, A_GROUP, A_HEAD_DIM) + shape[axis + 1:])
    return jnp.swapaxes(w, axis, axis + 1).reshape(shape)


def kernel(x, p, w_in, b_gate, conv_w, conv_b, mlstm_norm_g, attn_sink, w_out, ln1_g, ln1_b,
           w_ffn_in, w_ffn_out, ln2_g, ln2_b, w_ple_gate, w_ple_proj):
    batch, seq, d = x.shape
    depth = w_in.shape[0]
    t = batch * seq
    alpha = float((2 * depth) ** 0.25)
    assert seq % (CHUNK * M_OUT_GROUP) == 0 and seq >= 3 * A_BLOCK
    assert seq % (A_BLOCK * A_BLOCKS_PER_ITER) == 0
    assert t % INPROJ_TM == 0 and t % POST_TM == 0
    assert seq % INPROJ_TM == 0
    assert w_in.shape[2] == 4 * M_WIDTH + N_GATE + A_WIDTH + 2 * A_KV_WIDTH

    g0 = 4 * M_WIDTH
    aq0 = g0 + N_GATE
    akv0 = aq0 + A_WIDTH
    assert A_HEAD_DIM in (4 ** e for e in range(8)), "softmax scale must be a power of two to fold"
    w_in16 = w_in.astype(BF16)
    waq = _pair_heads(w_in16[:, :, aq0:akv0], 2) * jnp.asarray(A_HEAD_DIM ** -0.5, BF16)
    wakv = w_in16[:, :, akv0:]
    wgate = jnp.pad(w_in16[:, :, g0:aq0], ((0, 0), (0, 0), (0, G_WIDTH - N_GATE)))
    gate_bias = jnp.pad(b_gate, ((0, 0), (0, G_WIDTH - N_GATE)))[:, None, :]
    wom = w_out[:, :M_WIDTH].astype(BF16)
    woa = _pair_heads(w_out[:, M_WIDTH:], 1).astype(BF16)
    wgu = w_ffn_in.astype(BF16)
    wd = w_ffn_out.astype(BF16)
    wpg = w_ple_gate.astype(BF16)
    wpp = w_ple_proj.astype(BF16)
    conv_b3, norm_g3, ln1_g3, ln1_b3, ln2_g3, ln2_b3 = (
        a[:, None, :] for a in (conv_b, mlstm_norm_g, ln1_g, ln1_b, ln2_g, ln2_b))

    h = x.reshape(t, d)
    p3 = p.reshape(depth, t, p.shape[-1])
    for i in range(depth):
        qt, vt, u, gt = _inproj(i, h, seq, w_in16, waq, wakv, wgate, gate_bias, conv_w, conv_b3)
        rows = _gateprep(gt, batch, seq)
        hm = _mlstm(i, qt, vt, u, rows, norm_g3, batch, seq)
        ha = _attn(i, u, attn_sink, batch, seq)
        h = _post(i, alpha, h, hm, ha, p3, wom, woa, wgu, wd, wpg, wpp,
                  ln1_g3, ln1_b3, ln2_g3, ln2_b3)
    return h.reshape(batch, seq, d)
```

```python
import functools

import jax
import jax.numpy as jnp
from jax import lax
from jax.experimental import pallas as pl
from jax.experimental.pallas import tpu as pltpu

F32 = jnp.float32
BF16 = jnp.bfloat16

M_HEADS = 4
M_HEAD_DIM = 128
M_WIDTH = M_HEADS * M_HEAD_DIM
M_CONV = 5
N_GATE = 4 * M_HEADS
A_HEADS = 8
A_KV_HEADS = 2
A_HEAD_DIM = 64
A_GROUP = A_HEADS // A_KV_HEADS
A_WIDTH = A_HEADS * A_HEAD_DIM
A_KV_WIDTH = A_KV_HEADS * A_HEAD_DIM
WINDOW = 128
LN_EPS = 1e-5
MH_EPS = 1e-6

LANES = 128
F32_SUBLANES = 8
BF16_SUBLANES = 16

CHUNK = 256
A_BLOCK = 128
A_BLOCKS_PER_ITER = 8
U_K, U_O = 0, M_WIDTH
U_AQ = 2 * M_WIDTH
U_AKV = U_AQ + A_WIDTH
U_WIDTH = U_AKV + 2 * A_KV_WIDTH
G_WIDTH = LANES
INPROJ_COLS = 256
INPROJ_STRIP = 128
GATE_ROWS = 16
M_OUT_GROUP = 4
M_EXT = BF16_SUBLANES
M_VT_ROWS = M_HEAD_DIM + M_EXT
INPROJ_TM = 1024
POST_TM = 512
POST_SUB = 256
VMEM_LIMIT = 56 * 1024 * 1024


def _sigmoid(x):
    return 1.0 / (1.0 + jnp.exp(-x))


def _layer_norm(z, g, b):
    mu = jnp.mean(z, axis=-1, keepdims=True)
    zc = z - mu
    var = jnp.mean(zc * zc, axis=-1, keepdims=True)
    return zc * lax.rsqrt(var + LN_EPS) * g + b


def _layer_spec(stacked, layer, cols=None, col_block=0):
    _, rows, all_cols = stacked.shape
    return pl.BlockSpec((None, rows, cols or all_cols), lambda *_: (layer, 0, col_block),
                        pipeline_mode=pl.Buffered(1))


def _inproj_kernel(seq, x_ref, xp_ref, xn_ref, wqk_ref, wv_ref, wo_ref, waq_ref, wakv_ref, wg_ref,
                   gb_ref, cw_ref, cb_ref, qt_ref, vt_ref, u_ref, gt_ref, win_ref):
    tm = x_ref.shape[0]
    halo = F32_SUBLANES
    tiles_per_seq = seq // tm
    pos = pl.program_id(0) % tiles_per_seq
    xe = jnp.concatenate([x_ref[...], xp_ref[...], xn_ref[...]], axis=0).astype(BF16)
    xb = xe[:tm]
    cw = cw_ref[...]
    cb = cb_ref[...]

    def plain(dst, w_ref, w_cols):
        def run():
            dst[...] = jnp.dot(xb, w_ref[:, w_cols], preferred_element_type=F32).astype(dst.dtype)
        return run

    def project(slot, c0):
        cols = slice(c0, c0 + INPROJ_COLS)
        win = win_ref.at[slot]
        pre = jnp.dot(xe, wqk_ref[:, cols], preferred_element_type=F32)
        win[0:halo, :] = jnp.where(pos > 0, pre[tm:tm + halo], 0.0)
        win[halo:halo + tm, :] = pre[:tm]
        win[halo + tm:, :] = jnp.where(pos < tiles_per_seq - 1, pre[tm + halo:], 0.0)

    def taps(slot, c0):
        cols = slice(c0, c0 + INPROJ_COLS)
        win = win_ref.at[slot]
        for r0 in range(0, tm, INPROJ_STRIP):
            rows = slice(r0, r0 + INPROJ_STRIP)
            y = cb[:, cols]
            for k in range(M_CONV):
                first = r0 + halo + k - M_CONV // 2
                y = y + cw[k:k + 1, cols] * win[first:first + INPROJ_STRIP, :]
            y = y * _sigmoid(y)
            if c0 < M_WIDTH:
                qt_ref[cols, rows] = y.T.astype(BF16)
            else:
                u_ref[rows, U_K + c0 - M_WIDTH:U_K + c0 - M_WIDTH + INPROJ_COLS] = (
                    y * (M_HEAD_DIM ** -0.5)).astype(BF16)

    def v_cols(slot, c0):
        cols = slice(c0, c0 + INPROJ_COLS)
        win = win_ref.at[slot]
        win[0:tm, :] = jnp.dot(xb, wv_ref[:, cols], preferred_element_type=F32)
        for r0 in range(0, tm, INPROJ_STRIP):
            rows = slice(r0, r0 + INPROJ_STRIP)
            vt = win[rows, :].T.astype(BF16)
            for c in range(0, INPROJ_COLS, M_HEAD_DIM):
                base = (c0 + c) // M_HEAD_DIM * M_VT_ROWS
                vt_ref[base:base + M_HEAD_DIM, rows] = vt[c:c + M_HEAD_DIM]

    ext = (lax.broadcasted_iota(jnp.int32, (M_EXT, tm), 0) == 0).astype(BF16)
    for head in range(M_HEADS):
        vt_ref[head * M_VT_ROWS + M_HEAD_DIM:(head + 1) * M_VT_ROWS, :] = ext

    def gates():
        g = jnp.dot(xb, wg_ref[...], preferred_element_type=F32) + gb_ref[...]
        for r0 in range(0, tm, INPROJ_STRIP):
            rows = slice(r0, r0 + INPROJ_STRIP)
            gt_ref[:, rows] = g[rows].T[:N_GATE]

    def conv_cols(slot, c0):
        project(slot, c0)
        taps(slot, c0)

    half = (U_AQ - U_O) // 2
    plains = [plain(u_ref.at[:, U_O:U_O + half], wo_ref, slice(0, half)),
              plain(u_ref.at[:, U_O + half:U_AQ], wo_ref, slice(half, 2 * half)),
              plain(u_ref.at[:, U_AQ:U_AQ + half], waq_ref, slice(0, half)),
              plain(u_ref.at[:, U_AQ + half:U_AKV], waq_ref, slice(half, 2 * half)),
              plain(u_ref.at[:, U_AKV:], wakv_ref, slice(None)),
              gates]
    passes = ([functools.partial(conv_cols, i % 2, c0)
               for i, c0 in enumerate(range(0, 2 * M_WIDTH, INPROJ_COLS))]
              + [functools.partial(v_cols, i % 2, c0)
                 for i, c0 in enumerate(range(0, M_WIDTH, INPROJ_COLS))])
    assert len(passes) == len(plains)
    for run_pass, run_plain in zip(passes, plains):
        run_pass()
        run_plain()


def _inproj(layer, x2, seq, wmain, waq, wakv, wg, gate_bias, conv_w, conv_b):
    t, d = x2.shape
    tm = INPROJ_TM
    halo = F32_SUBLANES
    tail = [waq, wakv, wg, gate_bias, conv_w, conv_b]
    return pl.pallas_call(
        functools.partial(_inproj_kernel, seq),
        grid=(t // tm,),
        in_specs=[pl.BlockSpec((tm, d), lambda i: (i, 0)),
                  pl.BlockSpec((halo, d), lambda i: (jnp.maximum(i * (tm // halo) - 1, 0), 0)),
                  pl.BlockSpec((halo, d),
                               lambda i: (jnp.minimum((i + 1) * (tm // halo), t // halo - 1), 0)),
                  _layer_spec(wmain, layer, 2 * M_WIDTH, 0),
                  _layer_spec(wmain, layer, M_WIDTH, 2),
                  _layer_spec(wmain, layer, M_WIDTH, 3)]
                 + [_layer_spec(w, layer) for w in tail],
        out_specs=[pl.BlockSpec((M_WIDTH, tm), lambda i: (0, i)),
                   pl.BlockSpec((M_HEADS * M_VT_ROWS, tm), lambda i: (0, i)),
                   pl.BlockSpec((tm, U_WIDTH), lambda i: (i, 0)),
                   pl.BlockSpec((N_GATE, tm), lambda i: (0, i))],
        out_shape=[jax.ShapeDtypeStruct((M_WIDTH, t), BF16),
                   jax.ShapeDtypeStruct((M_HEADS * M_VT_ROWS, t), BF16),
                   jax.ShapeDtypeStruct((t, U_WIDTH), BF16),
                   jax.ShapeDtypeStruct((N_GATE, t), F32)],
        scratch_shapes=[pltpu.VMEM((2, tm + 2 * halo, INPROJ_COLS), F32)],
        compiler_params=pltpu.CompilerParams(dimension_semantics=("arbitrary",),
                                             vmem_limit_bytes=VMEM_LIMIT),
        name="inproj",
    )(x2, x2, x2, wmain, wmain, wmain, *tail)


def _gateprep_kernel(gt_ref, rows_ref, mp_ref):
    s = gt_ref.shape[1]
    nc = s // CHUNK
    nd = 2 * M_HEADS
    gt = gt_ref[...]
    ig = gt[0:nd]
    fg = gt[nd:2 * nd]
    lf = jnp.minimum(fg, 0.0) - jnp.log1p(jnp.exp(-jnp.abs(fg)))
    pos = lax.broadcasted_iota(jnp.int32, (nd, s), 1) & (CHUNK - 1)
    fwd_row = lax.broadcasted_iota(jnp.int32, (nd, s), 0) < M_HEADS

    def chunk_scans(x, op, identity):
        pre, suf = x, x
        sh = 1
        while sh < CHUNK:
            pre = op(pre, jnp.where(pos >= sh, pltpu.roll(pre, sh, 1), identity))
            suf = op(suf, jnp.where(pos < CHUNK - sh, pltpu.roll(suf, s - sh, 1), identity))
            sh *= 2
        return pre, suf

    lf_pre, lf_suf = chunk_scans(lf, jnp.add, 0.0)
    bcum = jnp.where(fwd_row, lf_pre, lf_suf)
    gtot = lf_pre + lf_suf - lf
    r = ig - bcum
    r_pre, r_suf = chunk_scans(r, jnp.maximum, -jnp.inf)
    rcummax = jnp.where(fwd_row, r_pre, r_suf)
    rmax = jnp.maximum(r_pre, r_suf)
    m = jnp.zeros((M_HEADS, CHUNK), F32)
    for c in range(nc):
        sl = slice(c * CHUNK, (c + 1) * CHUNK)
        mp_ref[0:M_HEADS, sl] = m
        m = gtot[0:M_HEADS, sl] + jnp.maximum(m, rmax[0:M_HEADS, sl])
    m = jnp.zeros((M_HEADS, CHUNK), F32)
    for c in reversed(range(nc)):
        sl = slice(c * CHUNK, (c + 1) * CHUNK)
        mp_ref[M_HEADS:nd, sl] = m
        m = gtot[M_HEADS:nd, sl] + jnp.maximum(m, rmax[M_HEADS:nd, sl])
    m_in = mp_ref[...]
    mm = jnp.maximum(m_in, rmax)
    mu = jnp.maximum(rcummax, m_in)
    table = (r, mu, bcum + mu, jnp.exp(m_in - mu), jnp.exp(r - mm), jnp.exp(m_in - mm))
    pad = jnp.zeros((GATE_ROWS - 2 * len(table), s), F32)
    for h in range(M_HEADS):
        pieces = []
        for arr in table:
            pieces += [arr[h:h + 1], arr[M_HEADS + h:M_HEADS + h + 1]]
        rows_ref[h] = jnp.concatenate(pieces + [pad], axis=0)


def _gateprep(gt, batch, seq):
    return pl.pallas_call(
        _gateprep_kernel,
        grid=(batch,),
        in_specs=[pl.BlockSpec((N_GATE, seq), lambda b: (0, b))],
        out_specs=pl.BlockSpec((None, M_HEADS, GATE_ROWS, seq), lambda b: (b, 0, 0, 0)),
        out_shape=jax.ShapeDtypeStruct((batch, M_HEADS, GATE_ROWS, seq), F32),
        scratch_shapes=[pltpu.VMEM((2 * M_HEADS, seq), F32)],
        compiler_params=pltpu.CompilerParams(dimension_semantics=("arbitrary",),
                                             vmem_limit_bytes=VMEM_LIMIT),
        name="gateprep",
    )(gt)


def _mlstm_kernel(qt_ref, vt_ref, ks_ref, o_ref, rows_ref, ng_ref, out_ref,
                  csf_ref, csb_ref, cf_ref, cb_ref):
    seq = ks_ref.shape[0]
    nc = seq // CHUNK
    dh = M_HEAD_DIM

    def gate_row(kind, rev, r0):
        return rows_ref[2 * kind + rev:2 * kind + rev + 1, pl.ds(r0, CHUNK)]

    def state_step(ci, rev, c_ref, cs_ref):
        r0 = pl.multiple_of(ci * CHUNK, CHUNK)
        cs_ref[ci] = c_ref[...].astype(BF16)
        vw = (vt_ref[:, pl.ds(r0, CHUNK)].astype(F32) * gate_row(4, rev, r0)).astype(BF16)
        decay = gate_row(5, rev, r0)[:, 0:1]
        c_ref[...] = decay * c_ref[...] + jnp.dot(vw, ks_ref[pl.ds(r0, CHUNK), :],
                                                  preferred_element_type=F32)

    cf_ref[...] = jnp.zeros_like(cf_ref)
    cb_ref[...] = jnp.zeros_like(cb_ref)

    def state_body(c, carry):
        state_step(c, 0, cf_ref, csf_ref)
        state_step(nc - 1 - c, 1, cb_ref, csb_ref)
        return carry

    lax.fori_loop(0, nc, state_body, 0, unroll=4)

    s_idx = lax.broadcasted_iota(jnp.int32, (LANES, LANES), 0)
    t_idx = lax.broadcasted_iota(jnp.int32, (LANES, LANES), 1)
    tri = (s_idx <= t_idx, s_idx >= t_idx)
    krow = lax.broadcasted_iota(jnp.int32, (BF16_SUBLANES, CHUNK), 0)
    nblk = CHUNK // LANES

    def intra_weights(a, qk, rev):
        cols = []
        for bt in range(nblk):
            col = []
            for bs in range(nblk):
                blk = (slice(bs * LANES, (bs + 1) * LANES), slice(bt * LANES, (bt + 1) * LANES))
                if bs == bt:
                    w = qk[blk] * jnp.exp(jnp.where(tri[rev], a[blk], -jnp.inf))
                elif (bs > bt) == bool(rev):
                    w = qk[blk] * jnp.exp(a[blk])
                else:
                    w = jnp.zeros((LANES, LANES), F32)
                col.append(w.astype(BF16))
            cols.append(jnp.concatenate(col, axis=0))
        return jnp.concatenate(cols, axis=1)

    def split_rows(x, first):
        hi = x.astype(BF16).astype(F32)
        mid = (x - hi).astype(BF16).astype(F32)
        lo = x - hi - mid
        out = jnp.where(krow == first, hi, jnp.where(krow == first + 1, mid,
                                                     jnp.where(krow == first + 2, lo, 0.0)))
        ones = (krow >= 3 - first) & (krow < 6 - first)
        return jnp.where(ones, 1.0, out).astype(BF16)

    dirs = ((0, csf_ref), (1, csb_ref))

    def out_body(g, carry):
        cs = [g * M_OUT_GROUP + i for i in range(M_OUT_GROUP)]
        r0s = [pl.multiple_of(c * CHUNK, CHUNK) for c in cs]
        qts = [qt_ref[:, pl.ds(r0, CHUNK)] for r0 in r0s]
        qks = [jnp.dot(ks_ref[pl.ds(r0, CHUNK), :], qt, preferred_element_type=F32)
               for r0, qt in zip(r0s, qts)]
        a_s = [[lax.dot_general(split_rows(gate_row(0, rev, r0), 0),
                                split_rows(-gate_row(1, rev, r0), 3),
                                (((0,), (0,)), ((), ())), preferred_element_type=F32)
                for rev, _ in dirs] for r0 in r0s]
        inters = [[jnp.dot(cs_ref[c], qt, preferred_element_type=F32) for _, cs_ref in dirs]
                  for c, qt in zip(cs, qts)]
        ws = [[intra_weights(a[rev], qk, rev) for rev, _ in dirs] for a, qk in zip(a_s, qks)]
        tots = [[jnp.dot(vt_ref[:, pl.ds(r0, CHUNK)], w[rev], preferred_element_type=F32)
                 + gate_row(3, rev, r0) * inter[rev] for rev, _ in dirs]
                for r0, w, inter in zip(r0s, ws, inters)]
        for r0, tot in zip(r0s, tots):
            h = None
            for rev, _ in dirs:
                den = jnp.maximum(jnp.abs(tot[rev][dh:dh + 1, :]), jnp.exp(-gate_row(2, rev, r0)))
                hd = tot[rev][:dh, :] * (1.0 / den)
                h = hd if h is None else h + hd
            hc = h - jnp.mean(h, axis=0, keepdims=True)
            hn = hc * lax.rsqrt(jnp.mean(hc * hc, axis=0, keepdims=True) + MH_EPS)
            og = _sigmoid(o_ref[pl.ds(r0, CHUNK), :].astype(F32))
            out_ref[pl.ds(r0, CHUNK), :] = (hn.T * ng_ref[...] * og).astype(BF16)
        return carry

    lax.fori_loop(0, nc // M_OUT_GROUP, out_body, 0)


def _mlstm(layer, qt, vt, u, rows, norm_g, batch, seq):
    dh = M_HEAD_DIM
    de = M_VT_ROWS
    nc = seq // CHUNK

    def tcol(rows):
        return pl.BlockSpec((rows, seq), lambda b, h: (h, b))

    def ucol(base):
        return pl.BlockSpec((seq, dh), lambda b, h: (b, base // dh + h))

    return pl.pallas_call(
        _mlstm_kernel,
        grid=(batch, M_HEADS),
        in_specs=[tcol(dh), tcol(de), ucol(U_K), ucol(U_O),
                  pl.BlockSpec((None, None, GATE_ROWS, seq), lambda b, h: (b, h, 0, 0)),
                  pl.BlockSpec((None, 1, dh), lambda b, h: (layer, 0, h))],
        out_specs=pl.BlockSpec((seq, dh), lambda b, h: (b, h)),
        out_shape=jax.ShapeDtypeStruct((batch * seq, M_WIDTH), BF16),
        scratch_shapes=[pltpu.VMEM((nc, de, dh), BF16),
                        pltpu.VMEM((nc, de, dh), BF16),
                        pltpu.VMEM((de, dh), F32),
                        pltpu.VMEM((de, dh), F32)],
        compiler_params=pltpu.CompilerParams(dimension_semantics=("arbitrary", "arbitrary"),
                                             vmem_limit_bytes=VMEM_LIMIT),
        name="mlstm",
    )(qt, vt, u, u, rows, norm_g)


A_DELTAS = (-A_BLOCK, 0, -2 * A_BLOCK)


def _attn_kernel(layer, sink_ref, q_ref, kv_ref, out_ref, bias_ref):
    seq = q_ref.shape[0]
    nb = seq // A_BLOCK
    span = 3 * A_BLOCK
    lo = lax.broadcasted_iota(jnp.int32, (A_BLOCK, LANES), 1) < A_HEAD_DIM
    lo_kv = lax.broadcasted_iota(jnp.int32, (span, LANES), 1) < A_HEAD_DIM
    sum_lane = lax.broadcasted_iota(jnp.int32, (span, LANES), 1)
    qi = lax.broadcasted_iota(jnp.int32, (A_BLOCK, span), 0)
    ki = lax.broadcasted_iota(jnp.int32, (A_BLOCK, span), 1)
    for variant, delta in enumerate(A_DELTAS):
        rel = jnp.abs(ki - qi + delta)
        base = jnp.where(rel <= WINDOW, -rel.astype(F32), -jnp.inf)
        for j in range(A_GROUP):
            for half in range(A_KV_HEADS):
                head = j + A_GROUP * half
                bias_ref[variant * A_GROUP + j, :, half * span:(half + 1) * span] = (
                    base * (2.0 ** -(head + 1)))

    def blocks(i, carry):
        q0s, kds, vds, variants = [], [], [], []
        for b in range(A_BLOCKS_PER_ITER):
            n = i * A_BLOCKS_PER_ITER + b
            q0 = pl.multiple_of(n * A_BLOCK, A_BLOCK)
            k0 = pl.multiple_of(jnp.clip(q0 - A_BLOCK, 0, seq - span), A_BLOCK)
            k3 = kv_ref[pl.ds(k0, span), 0:A_KV_WIDTH]
            v3 = kv_ref[pl.ds(k0, span), A_KV_WIDTH:2 * A_KV_WIDTH]
            zero = jnp.zeros_like(k3)
            q0s.append(q0)
            variants.append(jnp.where(n == 0, 1, jnp.where(n == nb - 1, 2, 0)))
            kds.append(jnp.concatenate([jnp.where(lo_kv, k3, zero), jnp.where(lo_kv, zero, k3)],
                                       axis=0))
            vds.append(jnp.concatenate(
                [jnp.concatenate([jnp.where(lo_kv, v3, zero), (sum_lane == 0).astype(BF16)], axis=1),
                 jnp.concatenate([jnp.where(lo_kv, zero, v3), (sum_lane == 1).astype(BF16)], axis=1)],
                axis=0))
        work = [(b, j) for b in range(A_BLOCKS_PER_ITER) for j in range(A_GROUP)]
        scs = [lax.dot_general(q_ref[pl.ds(q0s[b], A_BLOCK), j * LANES:(j + 1) * LANES], kds[b],
                               (((1,), (1,)), ((), ())), preferred_element_type=F32)
               + bias_ref[variants[b] * A_GROUP + j] for b, j in work]
        es, corrs = [], []
        for (b, j), sc in zip(work, scs):
            e, corr = [], []
            for half in range(A_KV_HEADS):
                sk = sink_ref[layer, j + A_GROUP * half]
                sch = sc[:, half * span:(half + 1) * span]
                mx = jnp.maximum(jnp.max(sch, axis=1, keepdims=True), sk)
                e.append(jnp.exp(sch - mx).astype(BF16))
                corr.append(jnp.exp(sk - mx))
            es.append(jnp.concatenate(e, axis=1))
            corrs.append(corr)
        pvs = [jnp.dot(e, vds[b], preferred_element_type=F32) for (b, j), e in zip(work, es)]
        for (b, j), pv, corr in zip(work, pvs, corrs):
            rden = [1.0 / (pv[:, LANES + half:LANES + half + 1] + corr[half])
                    for half in range(A_KV_HEADS)]
            out_ref[pl.ds(q0s[b], A_BLOCK), j * LANES:(j + 1) * LANES] = (
                pv[:, :LANES] * jnp.where(lo, rden[0], rden[1])).astype(BF16)
        return carry

    lax.fori_loop(0, nb // A_BLOCKS_PER_ITER, blocks, 0)


def _attn(layer, u, sink, batch, seq):
    return pl.pallas_call(
        functools.partial(_attn_kernel, layer),
        grid=(batch,),
        in_specs=[pl.BlockSpec(memory_space=pltpu.SMEM),
                  pl.BlockSpec((seq, A_WIDTH), lambda b: (b, U_AQ // A_WIDTH)),
                  pl.BlockSpec((seq, 2 * A_KV_WIDTH), lambda b: (b, U_AKV // (2 * A_KV_WIDTH)))],
        out_specs=pl.BlockSpec((seq, A_WIDTH), lambda b: (b, 0)),
        out_shape=jax.ShapeDtypeStruct((batch * seq, A_WIDTH), BF16),
        scratch_shapes=[pltpu.VMEM((len(A_DELTAS) * A_GROUP, A_BLOCK, 6 * A_BLOCK), F32)],
        compiler_params=pltpu.CompilerParams(dimension_semantics=("arbitrary",),
                                             vmem_limit_bytes=VMEM_LIMIT),
        name="attn",
    )(sink, u, u)


def _ff_chunks(d_ff):
    step = 1024
    return [(c0, min(c0 + step, d_ff)) for c0 in range(0, d_ff, step)]


def _post_kernel(alpha, x_ref, hm_ref, ha_ref, p_ref, wom_ref, woa_ref, wg_ref, wu_ref, wd_ref,
                 wpg_ref, wpp_ref, ln1g_ref, ln1b_ref, ln2g_ref, ln2b_ref, y_ref):
    subs = [slice(r0, r0 + POST_SUB) for r0 in range(0, x_ref.shape[0], POST_SUB)]
    x1s = []
    for rows in subs:
        mix = jnp.dot(hm_ref[rows, :], wom_ref[...], preferred_element_type=F32)
        mix = mix + jnp.dot(ha_ref[rows, :], woa_ref[...], preferred_element_type=F32)
        x1s.append(_layer_norm(alpha * x_ref[rows, :] + mix, ln1g_ref[...], ln1b_ref[...]))
    accs = []
    for rows, x1 in zip(subs, x1s):
        x1b = x1.astype(BF16)
        ple_gate = _sigmoid(jnp.dot(x1b, wpg_ref[...], preferred_element_type=F32))
        acc = alpha * x1 + ple_gate * jnp.dot(p_ref[rows, :].astype(BF16), wpp_ref[...],
                                              preferred_element_type=F32)
        for c0, c1 in _ff_chunks(wd_ref.shape[0]):
            gate = jnp.dot(x1b, wg_ref[:, c0:c1], preferred_element_type=F32)
            up = jnp.dot(x1b, wu_ref[:, c0:c1], preferred_element_type=F32)
            hid = (gate * _sigmoid(gate) * up).astype(BF16)
            acc = acc + jnp.dot(hid, wd_ref[c0:c1, :], preferred_element_type=F32)
        accs.append(acc)
    for rows, acc in zip(subs, accs):
        y_ref[rows, :] = _layer_norm(acc, ln2g_ref[...], ln2b_ref[...])


def _post(layer, alpha, x2, hm, ha, p3, wom, woa, wgu, wd, wpg, wpp, ln1g, ln1b, ln2g, ln2b):
    t, d = x2.shape
    tm = POST_TM
    d_ff = wd.shape[1]

    def rows(width):
        return pl.BlockSpec((tm, width), lambda i: (i, 0))

    tail = [wd, wpg, wpp, ln1g, ln1b, ln2g, ln2b]
    return pl.pallas_call(
        functools.partial(_post_kernel, alpha),
        grid=(t // tm,),
        in_specs=[rows(d), rows(hm.shape[1]), rows(ha.shape[1]),
                  pl.BlockSpec((None, tm, p3.shape[2]), lambda i: (layer, i, 0)),
                  _layer_spec(wom, layer), _layer_spec(woa, layer),
                  _layer_spec(wgu, layer, d_ff, 0), _layer_spec(wgu, layer, d_ff, 1)]
                 + [_layer_spec(w, layer) for w in tail],
        out_specs=rows(d),
        out_shape=jax.ShapeDtypeStruct((t, d), F32),
        compiler_params=pltpu.CompilerParams(dimension_semantics=("arbitrary",),
                                             vmem_limit_bytes=VMEM_LIMIT),
        name="post",
    )(x2, hm, ha, p3, wom, woa, wgu, wgu, *tail)


def _pair_heads(w, axis):
    shape = w.shape
    w = w.reshape(shape[:axis] + (A_KV_HEADS, A_GROUP, A_HEAD_DIM) + shape[axis + 1:])
    return jnp.swapaxes(w, axis, axis + 1).reshape(shape)


def kernel(x, p, w_in, b_gate, conv_w, conv_b, mlstm_norm_g, attn_sink, w_out, ln1_g, ln1_b,
           w_ffn_in, w_ffn_out, ln2_g, ln2_b, w_ple_gate, w_ple_proj):
    batch, seq, d = x.shape
    depth = w_in.shape[0]
    t = batch * seq
    alpha = float((2 * depth) ** 0.25)
    assert seq % (CHUNK * M_OUT_GROUP) == 0 and seq >= 3 * A_BLOCK
    assert seq % (A_BLOCK * A_BLOCKS_PER_ITER) == 0
    assert t % INPROJ_TM == 0 and t % POST_TM == 0
    assert seq % INPROJ_TM == 0
    assert w_in.shape[2] == 4 * M_WIDTH + N_GATE + A_WIDTH + 2 * A_KV_WIDTH

    g0 = 4 * M_WIDTH
    aq0 = g0 + N_GATE
    akv0 = aq0 + A_WIDTH
    assert A_HEAD_DIM in (4 ** e for e in range(8)), "softmax scale must be a power of two to fold"
    w_in16 = w_in.astype(BF16)
    waq = _pair_heads(w_in16[:, :, aq0:akv0], 2) * jnp.asarray(A_HEAD_DIM ** -0.5, BF16)
    wakv = w_in16[:, :, akv0:]
    wgate = jnp.pad(w_in16[:, :, g0:aq0], ((0, 0), (0, 0), (0, G_WIDTH - N_GATE)))
    gate_bias = jnp.pad(b_gate, ((0, 0), (0, G_WIDTH - N_GATE)))[:, None, :]
    wom = w_out[:, :M_WIDTH].astype(BF16)
    woa = _pair_heads(w_out[:, M_WIDTH:], 1).astype(BF16)
    wgu = w_ffn_in.astype(BF16)
    wd = w_ffn_out.astype(BF16)
    wpg = w_ple_gate.astype(BF16)
    wpp = w_ple_proj.astype(BF16)
    conv_b3, norm_g3, ln1_g3, ln1_b3, ln2_g3, ln2_b3 = (
        a[:, None, :] for a in (conv_b, mlstm_norm_g, ln1_g, ln1_b, ln2_g, ln2_b))

    h = x.reshape(t, d)
    p3 = p.reshape(depth, t, p.shape[-1])
    for i in range(depth):
        qt, vt, u, gt = _inproj(i, h, seq, w_in16, waq, wakv, wgate, gate_bias, conv_w, conv_b3)
        rows = _gateprep(gt, batch, seq)
        hm = _mlstm(i, qt, vt, u, rows, norm_g3, batch, seq)
        ha = _attn(i, u, attn_sink, batch, seq)
        h = _post(i, alpha, h, hm, ha, p3, wom, woa, wgu, wd, wpg, wpp,
                  ln1_g3, ln1_b3, ln2_g3, ln2_b3)
    return h.reshape(batch, seq, d)
```

```python
import functools

import jax
import jax.numpy as jnp
from jax import lax
from jax.experimental import pallas as pl
from jax.experimental.pallas import tpu as pltpu

F32 = jnp.float32
BF16 = jnp.bfloat16

M_HEADS = 4
M_HEAD_DIM = 128
M_WIDTH = M_HEADS * M_HEAD_DIM
M_CONV = 5
N_GATE = 4 * M_HEADS
A_HEADS = 8
A_KV_HEADS = 2
A_HEAD_DIM = 64
A_GROUP = A_HEADS // A_KV_HEADS
A_WIDTH = A_HEADS * A_HEAD_DIM
A_KV_WIDTH = A_KV_HEADS * A_HEAD_DIM
WINDOW = 128
LN_EPS = 1e-5
MH_EPS = 1e-6

LANES = 128
F32_SUBLANES = 8
BF16_SUBLANES = 16

CHUNK = 256
A_BLOCK = 128
A_BLOCKS_PER_ITER = 8
U_K, U_O = 0, M_WIDTH
U_AQ = 2 * M_WIDTH
U_AKV = U_AQ + A_WIDTH
U_WIDTH = U_AKV + 2 * A_KV_WIDTH
G_WIDTH = LANES
INPROJ_COLS = 256
INPROJ_STRIP = 128
GATE_ROWS = 16
M_OUT_GROUP = 4
M_EXT = BF16_SUBLANES
M_VT_ROWS = M_HEAD_DIM + M_EXT
INPROJ_TM = 1024
POST_TM = 1024
POST_SUB = 256
VMEM_LIMIT = 56 * 1024 * 1024
POST_VMEM_LIMIT = 61 * 1024 * 1024


def _sigmoid(x):
    return 1.0 / (1.0 + jnp.exp(-x))


def _layer_norm(z, g, b):
    mu = jnp.mean(z, axis=-1, keepdims=True)
    zc = z - mu
    var = jnp.mean(zc * zc, axis=-1, keepdims=True)
    return zc * lax.rsqrt(var + LN_EPS) * g + b


def _layer_spec(stacked, layer, cols=None, col_block=0):
    _, rows, all_cols = stacked.shape
    return pl.BlockSpec((None, rows, cols or all_cols), lambda *_: (layer, 0, col_block),
                        pipeline_mode=pl.Buffered(1))


def _inproj_kernel(seq, x_ref, xp_ref, xn_ref, wqk_ref, wv_ref, wo_ref, waq_ref, wakv_ref, wg_ref,
                   gb_ref, cw_ref, cb_ref, qt_ref, vt_ref, u_ref, gt_ref, win_ref):
    tm = x_ref.shape[0]
    halo = F32_SUBLANES
    tiles_per_seq = seq // tm
    pos = pl.program_id(0) % tiles_per_seq
    xe = jnp.concatenate([x_ref[...], xp_ref[...], xn_ref[...]], axis=0).astype(BF16)
    xb = xe[:tm]
    cw = cw_ref[...]
    cb = cb_ref[...]

    def plain(dst, w_ref, w_cols):
        def run():
            dst[...] = jnp.dot(xb, w_ref[:, w_cols], preferred_element_type=F32).astype(dst.dtype)
        return run

    def project(slot, c0):
        cols = slice(c0, c0 + INPROJ_COLS)
        win = win_ref.at[slot]
        pre = jnp.dot(xe, wqk_ref[:, cols], preferred_element_type=F32)
        win[0:halo, :] = jnp.where(pos > 0, pre[tm:tm + halo], 0.0)
        win[halo:halo + tm, :] = pre[:tm]
        win[halo + tm:, :] = jnp.where(pos < tiles_per_seq - 1, pre[tm + halo:], 0.0)

    def taps(slot, c0):
        cols = slice(c0, c0 + INPROJ_COLS)
        win = win_ref.at[slot]
        for r0 in range(0, tm, INPROJ_STRIP):
            rows = slice(r0, r0 + INPROJ_STRIP)
            y = cb[:, cols]
            for k in range(M_CONV):
                first = r0 + halo + k - M_CONV // 2
                y = y + cw[k:k + 1, cols] * win[first:first + INPROJ_STRIP, :]
            y = y * _sigmoid(y)
            if c0 < M_WIDTH:
                qt_ref[cols, rows] = y.T.astype(BF16)
            else:
                u_ref[rows, U_K + c0 - M_WIDTH:U_K + c0 - M_WIDTH + INPROJ_COLS] = (
                    y * (M_HEAD_DIM ** -0.5)).astype(BF16)

    def v_cols(slot, c0):
        cols = slice(c0, c0 + INPROJ_COLS)
        win = win_ref.at[slot]
        win[0:tm, :] = jnp.dot(xb, wv_ref[:, cols], preferred_element_type=F32)
        for r0 in range(0, tm, INPROJ_STRIP):
            rows = slice(r0, r0 + INPROJ_STRIP)
            vt = win[rows, :].T.astype(BF16)
            for c in range(0, INPROJ_COLS, M_HEAD_DIM):
                base = (c0 + c) // M_HEAD_DIM * M_VT_ROWS
                vt_ref[base:base + M_HEAD_DIM, rows] = vt[c:c + M_HEAD_DIM]

    ext = (lax.broadcasted_iota(jnp.int32, (M_EXT, tm), 0) == 0).astype(BF16)
    for head in range(M_HEADS):
        vt_ref[head * M_VT_ROWS + M_HEAD_DIM:(head + 1) * M_VT_ROWS, :] = ext

    def gates():
        g = jnp.dot(xb, wg_ref[...], preferred_element_type=F32) + gb_ref[...]
        for r0 in range(0, tm, INPROJ_STRIP):
            rows = slice(r0, r0 + INPROJ_STRIP)
            gt_ref[:, rows] = g[rows].T[:N_GATE]

    def conv_cols(slot, c0):
        project(slot, c0)
        taps(slot, c0)

    half = (U_AQ - U_O) // 2
    plains = [plain(u_ref.at[:, U_O:U_O + half], wo_ref, slice(0, half)),
              plain(u_ref.at[:, U_O + half:U_AQ], wo_ref, slice(half, 2 * half)),
              plain(u_ref.at[:, U_AQ:U_AQ + half], waq_ref, slice(0, half)),
              plain(u_ref.at[:, U_AQ + half:U_AKV], waq_ref, slice(half, 2 * half)),
              plain(u_ref.at[:, U_AKV:], wakv_ref, slice(None)),
              gates]
    passes = ([functools.partial(conv_cols, i % 2, c0)
               for i, c0 in enumerate(range(0, 2 * M_WIDTH, INPROJ_COLS))]
              + [functools.partial(v_cols, i % 2, c0)
                 for i, c0 in enumerate(range(0, M_WIDTH, INPROJ_COLS))])
    assert len(passes) == len(plains)
    for run_pass, run_plain in zip(passes, plains):
        run_pass()
        run_plain()


def _inproj(layer, x2, seq, wmain, waq, wakv, wg, gate_bias, conv_w, conv_b):
    t, d = x2.shape
    tm = INPROJ_TM
    halo = F32_SUBLANES
    tail = [waq, wakv, wg, gate_bias, conv_w, conv_b]
    return pl.pallas_call(
        functools.partial(_inproj_kernel, seq),
        grid=(t // tm,),
        in_specs=[pl.BlockSpec((tm, d), lambda i: (i, 0)),
                  pl.BlockSpec((halo, d), lambda i: (jnp.maximum(i * (tm // halo) - 1, 0), 0)),
                  pl.BlockSpec((halo, d),
                               lambda i: (jnp.minimum((i + 1) * (tm // halo), t // halo - 1), 0)),
                  _layer_spec(wmain, layer, 2 * M_WIDTH, 0),
                  _layer_spec(wmain, layer, M_WIDTH, 2),
                  _layer_spec(wmain, layer, M_WIDTH, 3)]
                 + [_layer_spec(w, layer) for w in tail],
        out_specs=[pl.BlockSpec((M_WIDTH, tm), lambda i: (0, i)),
                   pl.BlockSpec((M_HEADS * M_VT_ROWS, tm), lambda i: (0, i)),
                   pl.BlockSpec((tm, U_WIDTH), lambda i: (i, 0)),
                   pl.BlockSpec((N_GATE, tm), lambda i: (0, i))],
        out_shape=[jax.ShapeDtypeStruct((M_WIDTH, t), BF16),
                   jax.ShapeDtypeStruct((M_HEADS * M_VT_ROWS, t), BF16),
                   jax.ShapeDtypeStruct((t, U_WIDTH), BF16),
                   jax.ShapeDtypeStruct((N_GATE, t), F32)],
        scratch_shapes=[pltpu.VMEM((2, tm + 2 * halo, INPROJ_COLS), F32)],
        compiler_params=pltpu.CompilerParams(dimension_semantics=("arbitrary",),
                                             vmem_limit_bytes=VMEM_LIMIT),
        name="inproj",
    )(x2, x2, x2, wmain, wmain, wmain, *tail)


def _gateprep_kernel(gt_ref, rows_ref, mp_ref):
    s = gt_ref.shape[1]
    nc = s // CHUNK
    nd = 2 * M_HEADS
    gt = gt_ref[...]
    ig = gt[0:nd]
    fg = gt[nd:2 * nd]
    lf = jnp.minimum(fg, 0.0) - jnp.log1p(jnp.exp(-jnp.abs(fg)))
    pos = lax.broadcasted_iota(jnp.int32, (nd, s), 1) & (CHUNK - 1)
    fwd_row = lax.broadcasted_iota(jnp.int32, (nd, s), 0) < M_HEADS

    def chunk_scans(x, op, identity):
        pre, suf = x, x
        sh = 1
        while sh < CHUNK:
            pre = op(pre, jnp.where(pos >= sh, pltpu.roll(pre, sh, 1), identity))
            suf = op(suf, jnp.where(pos < CHUNK - sh, pltpu.roll(suf, s - sh, 1), identity))
            sh *= 2
        return pre, suf

    lf_pre, lf_suf = chunk_scans(lf, jnp.add, 0.0)
    bcum = jnp.where(fwd_row, lf_pre, lf_suf)
    gtot = lf_pre + lf_suf - lf
    r = ig - bcum
    r_pre, r_suf = chunk_scans(r, jnp.maximum, -jnp.inf)
    rcummax = jnp.where(fwd_row, r_pre, r_suf)
    rmax = jnp.maximum(r_pre, r_suf)
    m = jnp.zeros((M_HEADS, CHUNK), F32)
    for c in range(nc):
        sl = slice(c * CHUNK, (c + 1) * CHUNK)
        mp_ref[0:M_HEADS, sl] = m
        m = gtot[0:M_HEADS, sl] + jnp.maximum(m, rmax[0:M_HEADS, sl])
    m = jnp.zeros((M_HEADS, CHUNK), F32)
    for c in reversed(range(nc)):
        sl = slice(c * CHUNK, (c + 1) * CHUNK)
        mp_ref[M_HEADS:nd, sl] = m
        m = gtot[M_HEADS:nd, sl] + jnp.maximum(m, rmax[M_HEADS:nd, sl])
    m_in = mp_ref[...]
    mm = jnp.maximum(m_in, rmax)
    mu = jnp.maximum(rcummax, m_in)
    table = (r, mu, bcum + mu, jnp.exp(m_in - mu), jnp.exp(r - mm), jnp.exp(m_in - mm))
    pad = jnp.zeros((GATE_ROWS - 2 * len(table), s), F32)
    for h in range(M_HEADS):
        pieces = []
        for arr in table:
            pieces += [arr[h:h + 1], arr[M_HEADS + h:M_HEADS + h + 1]]
        rows_ref[h] = jnp.concatenate(pieces + [pad], axis=0)


def _gateprep(gt, batch, seq):
    return pl.pallas_call(
        _gateprep_kernel,
        grid=(batch,),
        in_specs=[pl.BlockSpec((N_GATE, seq), lambda b: (0, b))],
        out_specs=pl.BlockSpec((None, M_HEADS, GATE_ROWS, seq), lambda b: (b, 0, 0, 0)),
        out_shape=jax.ShapeDtypeStruct((batch, M_HEADS, GATE_ROWS, seq), F32),
        scratch_shapes=[pltpu.VMEM((2 * M_HEADS, seq), F32)],
        compiler_params=pltpu.CompilerParams(dimension_semantics=("arbitrary",),
                                             vmem_limit_bytes=VMEM_LIMIT),
        name="gateprep",
    )(gt)


def _mlstm_kernel(qt_ref, vt_ref, ks_ref, o_ref, rows_ref, ng_ref, out_ref,
                  csf_ref, csb_ref, cf_ref, cb_ref):
    seq = ks_ref.shape[0]
    nc = seq // CHUNK
    dh = M_HEAD_DIM

    def gate_row(kind, rev, r0):
        return rows_ref[2 * kind + rev:2 * kind + rev + 1, pl.ds(r0, CHUNK)]

    def state_step(ci, rev, c_ref, cs_ref):
        r0 = pl.multiple_of(ci * CHUNK, CHUNK)
        cs_ref[ci] = c_ref[...].astype(BF16)
        vw = (vt_ref[:, pl.ds(r0, CHUNK)].astype(F32) * gate_row(4, rev, r0)).astype(BF16)
        decay = gate_row(5, rev, r0)[:, 0:1]
        c_ref[...] = decay * c_ref[...] + jnp.dot(vw, ks_ref[pl.ds(r0, CHUNK), :],
                                                  preferred_element_type=F32)

    cf_ref[...] = jnp.zeros_like(cf_ref)
    cb_ref[...] = jnp.zeros_like(cb_ref)

    def state_body(c, carry):
        state_step(c, 0, cf_ref, csf_ref)
        state_step(nc - 1 - c, 1, cb_ref, csb_ref)
        return carry

    lax.fori_loop(0, nc, state_body, 0, unroll=4)

    s_idx = lax.broadcasted_iota(jnp.int32, (LANES, LANES), 0)
    t_idx = lax.broadcasted_iota(jnp.int32, (LANES, LANES), 1)
    tri = (s_idx <= t_idx, s_idx >= t_idx)
    krow = lax.broadcasted_iota(jnp.int32, (BF16_SUBLANES, CHUNK), 0)
    nblk = CHUNK // LANES

    def intra_weights(a, qk, rev):
        cols = []
        for bt in range(nblk):
            col = []
            for bs in range(nblk):
                blk = (slice(bs * LANES, (bs + 1) * LANES), slice(bt * LANES, (bt + 1) * LANES))
                if bs == bt:
                    w = qk[blk] * jnp.exp(jnp.where(tri[rev], a[blk], -jnp.inf))
                elif (bs > bt) == bool(rev):
                    w = qk[blk] * jnp.exp(a[blk])
                else:
                    w = jnp.zeros((LANES, LANES), F32)
                col.append(w.astype(BF16))
            cols.append(jnp.concatenate(col, axis=0))
        return jnp.concatenate(cols, axis=1)

    def split_rows(x, first):
        hi = x.astype(BF16).astype(F32)
        mid = (x - hi).astype(BF16).astype(F32)
        lo = x - hi - mid
        out = jnp.where(krow == first, hi, jnp.where(krow == first + 1, mid,
                                                     jnp.where(krow == first + 2, lo, 0.0)))
        ones = (krow >= 3 - first) & (krow < 6 - first)
        return jnp.where(ones, 1.0, out).astype(BF16)

    dirs = ((0, csf_ref), (1, csb_ref))

    def out_body(g, carry):
        cs = [g * M_OUT_GROUP + i for i in range(M_OUT_GROUP)]
        r0s = [pl.multiple_of(c * CHUNK, CHUNK) for c in cs]
        qts = [qt_ref[:, pl.ds(r0, CHUNK)] for r0 in r0s]
        qks = [jnp.dot(ks_ref[pl.ds(r0, CHUNK), :], qt, preferred_element_type=F32)
               for r0, qt in zip(r0s, qts)]
        a_s = [[lax.dot_general(split_rows(gate_row(0, rev, r0), 0),
                                split_rows(-gate_row(1, rev, r0), 3),
                                (((0,), (0,)), ((), ())), preferred_element_type=F32)
                for rev, _ in dirs] for r0 in r0s]
        inters = [[jnp.dot(cs_ref[c], qt, preferred_element_type=F32) for _, cs_ref in dirs]
                  for c, qt in zip(cs, qts)]
        ws = [[intra_weights(a[rev], qk, rev) for rev, _ in dirs] for a, qk in zip(a_s, qks)]
        tots = [[jnp.dot(vt_ref[:, pl.ds(r0, CHUNK)], w[rev], preferred_element_type=F32)
                 + gate_row(3, rev, r0) * inter[rev] for rev, _ in dirs]
                for r0, w, inter in zip(r0s, ws, inters)]
        for r0, tot in zip(r0s, tots):
            h = None
            for rev, _ in dirs:
                den = jnp.maximum(jnp.abs(tot[rev][dh:dh + 1, :]), jnp.exp(-gate_row(2, rev, r0)))
                hd = tot[rev][:dh, :] * (1.0 / den)
                h = hd if h is None else h + hd
            hc = h - jnp.mean(h, axis=0, keepdims=True)
            hn = hc * lax.rsqrt(jnp.mean(hc * hc, axis=0, keepdims=True) + MH_EPS)
            og = _sigmoid(o_ref[pl.ds(r0, CHUNK), :].astype(F32))
            out_ref[pl.ds(r0, CHUNK), :] = (hn.T * ng_ref[...] * og).astype(BF16)
        return carry

    lax.fori_loop(0, nc // M_OUT_GROUP, out_body, 0)


def _mlstm(layer, qt, vt, u, rows, norm_g, batch, seq):
    dh = M_HEAD_DIM
    de = M_VT_ROWS
    nc = seq // CHUNK

    def tcol(rows):
        return pl.BlockSpec((rows, seq), lambda b, h: (h, b))

    def ucol(base):
        return pl.BlockSpec((seq, dh), lambda b, h: (b, base // dh + h))

    return pl.pallas_call(
        _mlstm_kernel,
        grid=(batch, M_HEADS),
        in_specs=[tcol(dh), tcol(de), ucol(U_K), ucol(U_O),
                  pl.BlockSpec((None, None, GATE_ROWS, seq), lambda b, h: (b, h, 0, 0)),
                  pl.BlockSpec((None, 1, dh), lambda b, h: (layer, 0, h))],
        out_specs=pl.BlockSpec((seq, dh), lambda b, h: (b, h)),
        out_shape=jax.ShapeDtypeStruct((batch * seq, M_WIDTH), BF16),
        scratch_shapes=[pltpu.VMEM((nc, de, dh), BF16),
                        pltpu.VMEM((nc, de, dh), BF16),
                        pltpu.VMEM((de, dh), F32),
                        pltpu.VMEM((de, dh), F32)],
        compiler_params=pltpu.CompilerParams(dimension_semantics=("arbitrary", "arbitrary"),
                                             vmem_limit_bytes=VMEM_LIMIT),
        name="mlstm",
    )(qt, vt, u, u, rows, norm_g)


A_DELTAS = (-A_BLOCK, 0, -2 * A_BLOCK)


def _attn_kernel(layer, sink_ref, q_ref, kv_ref, out_ref, bias_ref):
    seq = q_ref.shape[0]
    nb = seq // A_BLOCK
    span = 3 * A_BLOCK
    lo = lax.broadcasted_iota(jnp.int32, (A_BLOCK, LANES), 1) < A_HEAD_DIM
    lo_kv = lax.broadcasted_iota(jnp.int32, (span, LANES), 1) < A_HEAD_DIM
    sum_lane = lax.broadcasted_iota(jnp.int32, (span, LANES), 1)
    qi = lax.broadcasted_iota(jnp.int32, (A_BLOCK, span), 0)
    ki = lax.broadcasted_iota(jnp.int32, (A_BLOCK, span), 1)
    for variant, delta in enumerate(A_DELTAS):
        rel = jnp.abs(ki - qi + delta)
        base = jnp.where(rel <= WINDOW, -rel.astype(F32), -jnp.inf)
        for j in range(A_GROUP):
            for half in range(A_KV_HEADS):
                head = j + A_GROUP * half
                bias_ref[variant * A_GROUP + j, :, half * span:(half + 1) * span] = (
                    base * (2.0 ** -(head + 1)))

    def blocks(i, carry):
        q0s, kds, vds, variants = [], [], [], []
        for b in range(A_BLOCKS_PER_ITER):
            n = i * A_BLOCKS_PER_ITER + b
            q0 = pl.multiple_of(n * A_BLOCK, A_BLOCK)
            k0 = pl.multiple_of(jnp.clip(q0 - A_BLOCK, 0, seq - span), A_BLOCK)
            k3 = kv_ref[pl.ds(k0, span), 0:A_KV_WIDTH]
            v3 = kv_ref[pl.ds(k0, span), A_KV_WIDTH:2 * A_KV_WIDTH]
            zero = jnp.zeros_like(k3)
            q0s.append(q0)
            variants.append(jnp.where(n == 0, 1, jnp.where(n == nb - 1, 2, 0)))
            kds.append(jnp.concatenate([jnp.where(lo_kv, k3, zero), jnp.where(lo_kv, zero, k3)],
                                       axis=0))
            vds.append(jnp.concatenate(
                [jnp.concatenate([jnp.where(lo_kv, v3, zero), (sum_lane == 0).astype(BF16)], axis=1),
                 jnp.concatenate([jnp.where(lo_kv, zero, v3), (sum_lane == 1).astype(BF16)], axis=1)],
                axis=0))
        work = [(b, j) for b in range(A_BLOCKS_PER_ITER) for j in range(A_GROUP)]
        scs = [lax.dot_general(q_ref[pl.ds(q0s[b], A_BLOCK), j * LANES:(j + 1) * LANES], kds[b],
                               (((1,), (1,)), ((), ())), preferred_element_type=F32)
               + bias_ref[variants[b] * A_GROUP + j] for b, j in work]
        es, corrs = [], []
        for (b, j), sc in zip(work, scs):
            e, corr = [], []
            for half in range(A_KV_HEADS):
                sk = sink_ref[layer, j + A_GROUP * half]
                sch = sc[:, half * span:(half + 1) * span]
                mx = jnp.maximum(jnp.max(sch, axis=1, keepdims=True), sk)
                e.append(jnp.exp(sch - mx).astype(BF16))
                corr.append(jnp.exp(sk - mx))
            es.append(jnp.concatenate(e, axis=1))
            corrs.append(corr)
        pvs = [jnp.dot(e, vds[b], preferred_element_type=F32) for (b, j), e in zip(work, es)]
        for (b, j), pv, corr in zip(work, pvs, corrs):
            rden = [1.0 / (pv[:, LANES + half:LANES + half + 1] + corr[half])
                    for half in range(A_KV_HEADS)]
            out_ref[pl.ds(q0s[b], A_BLOCK), j * LANES:(j + 1) * LANES] = (
                pv[:, :LANES] * jnp.where(lo, rden[0], rden[1])).astype(BF16)
        return carry

    lax.fori_loop(0, nb // A_BLOCKS_PER_ITER, blocks, 0)


def _attn(layer, u, sink, batch, seq):
    return pl.pallas_call(
        functools.partial(_attn_kernel, layer),
        grid=(batch,),
        in_specs=[pl.BlockSpec(memory_space=pltpu.SMEM),
                  pl.BlockSpec((seq, A_WIDTH), lambda b: (b, U_AQ // A_WIDTH)),
                  pl.BlockSpec((seq, 2 * A_KV_WIDTH), lambda b: (b, U_AKV // (2 * A_KV_WIDTH)))],
        out_specs=pl.BlockSpec((seq, A_WIDTH), lambda b: (b, 0)),
        out_shape=jax.ShapeDtypeStruct((batch * seq, A_WIDTH), BF16),
        scratch_shapes=[pltpu.VMEM((len(A_DELTAS) * A_GROUP, A_BLOCK, 6 * A_BLOCK), F32)],
        compiler_params=pltpu.CompilerParams(dimension_semantics=("arbitrary",),
                                             vmem_limit_bytes=VMEM_LIMIT),
        name="attn",
    )(sink, u, u)


def _ff_chunks(d_ff):
    step = 1024
    return [(c0, min(c0 + step, d_ff)) for c0 in range(0, d_ff, step)]


def _post_kernel(alpha, x_ref, hm_ref, ha_ref, p_ref, wom_ref, woa_ref, wg_ref, wu_ref, wd_ref,
                 wpg_ref, wpp_ref, ln1g_ref, ln1b_ref, ln2g_ref, ln2b_ref, y_ref):
    subs = [slice(r0, r0 + POST_SUB) for r0 in range(0, x_ref.shape[0], POST_SUB)]
    n = len(subs)
    x1s, accs = {}, {}

    def mix_ln1(i):
        rows = subs[i]
        mix = jnp.dot(hm_ref[rows, :], wom_ref[...], preferred_element_type=F32)
        mix = mix + jnp.dot(ha_ref[rows, :], woa_ref[...], preferred_element_type=F32)
        x1s[i] = _layer_norm(alpha * x_ref[rows, :] + mix, ln1g_ref[...], ln1b_ref[...])

    def ffn_ple(i):
        x1 = x1s.pop(i)
        x1b = x1.astype(BF16)
        ple_gate = _sigmoid(jnp.dot(x1b, wpg_ref[...], preferred_element_type=F32))
        acc = alpha * x1 + ple_gate * jnp.dot(p_ref[subs[i], :].astype(BF16), wpp_ref[...],
                                              preferred_element_type=F32)
        for c0, c1 in _ff_chunks(wd_ref.shape[0]):
            gate = jnp.dot(x1b, wg_ref[:, c0:c1], preferred_element_type=F32)
            up = jnp.dot(x1b, wu_ref[:, c0:c1], preferred_element_type=F32)
            hid = (gate * _sigmoid(gate) * up).astype(BF16)
            acc = acc + jnp.dot(hid, wd_ref[c0:c1, :], preferred_element_type=F32)
        accs[i] = acc

    def ln2(i):
        y_ref[subs[i], :] = _layer_norm(accs.pop(i), ln2g_ref[...], ln2b_ref[...])

    for i in range(min(2, n)):
        mix_ln1(i)
    for i in range(n):
        ffn_ple(i)
        if i + 2 < n:
            mix_ln1(i + 2)
        if i >= 1:
            ln2(i - 1)
    ln2(n - 1)


def _post(layer, alpha, x2, hm, ha, p3, wom, woa, wgu, wd, wpg, wpp, ln1g, ln1b, ln2g, ln2b):
    t, d = x2.shape
    tm = POST_TM
    d_ff = wd.shape[1]

    def rows(width):
        return pl.BlockSpec((tm, width), lambda i: (i, 0))

    tail = [wd, wpg, wpp, ln1g, ln1b, ln2g, ln2b]
    return pl.pallas_call(
        functools.partial(_post_kernel, alpha),
        grid=(t // tm,),
        in_specs=[rows(d), rows(hm.shape[1]), rows(ha.shape[1]),
                  pl.BlockSpec((None, tm, p3.shape[2]), lambda i: (layer, i, 0)),
                  _layer_spec(wom, layer), _layer_spec(woa, layer),
                  _layer_spec(wgu, layer, d_ff, 0), _layer_spec(wgu, layer, d_ff, 1)]
                 + [_layer_spec(w, layer) for w in tail],
        out_specs=rows(d),
        out_shape=jax.ShapeDtypeStruct((t, d), F32),
        compiler_params=pltpu.CompilerParams(dimension_semantics=("arbitrary",),
                                             vmem_limit_bytes=POST_VMEM_LIMIT),
        name="post",
    )(x2, hm, ha, p3, wom, woa, wgu, wgu, *tail)


def _pair_heads(w, axis):
    shape = w.shape
    w = w.reshape(shape[:axis] + (A_KV_HEADS, A_GROUP, A_HEAD_DIM) + shape[axis + 1:])
    return jnp.swapaxes(w, axis, axis + 1).reshape(shape)


def kernel(x, p, w_in, b_gate, conv_w, conv_b, mlstm_norm_g, attn_sink, w_out, ln1_g, ln1_b,
           w_ffn_in, w_ffn_out, ln2_g, ln2_b, w_ple_gate, w_ple_proj):
    batch, seq, d = x.shape
    depth = w_in.shape[0]
    t = batch * seq
    alpha = float((2 * depth) ** 0.25)
    assert seq % (CHUNK * M_OUT_GROUP) == 0 and seq >= 3 * A_BLOCK
    assert seq % (A_BLOCK * A_BLOCKS_PER_ITER) == 0
    assert t % INPROJ_TM == 0 and t % POST_TM == 0
    assert seq % INPROJ_TM == 0
    assert w_in.shape[2] == 4 * M_WIDTH + N_GATE + A_WIDTH + 2 * A_KV_WIDTH

    g0 = 4 * M_WIDTH
    aq0 = g0 + N_GATE
    akv0 = aq0 + A_WIDTH
    assert A_HEAD_DIM in (4 ** e for e in range(8)), "softmax scale must be a power of two to fold"
    w_in16 = w_in.astype(BF16)
    waq = _pair_heads(w_in16[:, :, aq0:akv0], 2) * jnp.asarray(A_HEAD_DIM ** -0.5, BF16)
    wakv = w_in16[:, :, akv0:]
    wgate = jnp.pad(w_in16[:, :, g0:aq0], ((0, 0), (0, 0), (0, G_WIDTH - N_GATE)))
    gate_bias = jnp.pad(b_gate, ((0, 0), (0, G_WIDTH - N_GATE)))[:, None, :]
    wom = w_out[:, :M_WIDTH].astype(BF16)
    woa = _pair_heads(w_out[:, M_WIDTH:], 1).astype(BF16)
    wgu = w_ffn_in.astype(BF16)
    wd = w_ffn_out.astype(BF16)
    wpg = w_ple_gate.astype(BF16)
    wpp = w_ple_proj.astype(BF16)
    conv_b3, norm_g3, ln1_g3, ln1_b3, ln2_g3, ln2_b3 = (
        a[:, None, :] for a in (conv_b, mlstm_norm_g, ln1_g, ln1_b, ln2_g, ln2_b))

    h = x.reshape(t, d)
    p3 = p.reshape(depth, t, p.shape[-1])
    for i in range(depth):
        qt, vt, u, gt = _inproj(i, h, seq, w_in16, waq, wakv, wgate, gate_bias, conv_w, conv_b3)
        rows = _gateprep(gt, batch, seq)
        hm = _mlstm(i, qt, vt, u, rows, norm_g3, batch, seq)
        ha = _attn(i, u, attn_sink, batch, seq)
        h = _post(i, alpha, h, hm, ha, p3, wom, woa, wgu, wd, wpg, wpp,
                  ln1_g3, ln1_b3, ln2_g3, ln2_b3)
    return h.reshape(batch, seq, d)
```

```python
import functools

import jax
import jax.numpy as jnp
from jax import lax
from jax.experimental import pallas as pl
from jax.experimental.pallas import tpu as pltpu

F32 = jnp.float32
BF16 = jnp.bfloat16

M_HEADS = 4
M_HEAD_DIM = 128
M_WIDTH = M_HEADS * M_HEAD_DIM
M_CONV = 5
N_GATE = 4 * M_HEADS
A_HEADS = 8
A_KV_HEADS = 2
A_HEAD_DIM = 64
A_GROUP = A_HEADS // A_KV_HEADS
A_WIDTH = A_HEADS * A_HEAD_DIM
A_KV_WIDTH = A_KV_HEADS * A_HEAD_DIM
WINDOW = 128
LN_EPS = 1e-5
MH_EPS = 1e-6

LANES = 128
F32_SUBLANES = 8
BF16_SUBLANES = 16

CHUNK = 256
A_BLOCK = 128
A_BLOCKS_PER_ITER = 8
U_K, U_O = 0, M_WIDTH
U_AQ = 2 * M_WIDTH
U_AKV = U_AQ + A_WIDTH
U_WIDTH = U_AKV + 2 * A_KV_WIDTH
G_WIDTH = LANES
INPROJ_TM = 1024
INPROJ_COLS = 256
INPROJ_STRIP = 128
GATE_ROWS = 16
M_OUT_GROUP = 4
M_EXT = BF16_SUBLANES
M_VT_ROWS = M_HEAD_DIM + M_EXT
POST_TM = 512
POST_SUB = 256
VMEM_LIMIT = 56 * 1024 * 1024


def _sigmoid(x):
    return 1.0 / (1.0 + jnp.exp(-x))


def _layer_norm(z, g, b):
    mu = jnp.mean(z, axis=-1, keepdims=True)
    zc = z - mu
    var = jnp.mean(zc * zc, axis=-1, keepdims=True)
    return zc * lax.rsqrt(var + LN_EPS) * g + b


def _layer_spec(stacked, layer, cols=None, col_block=0):
    _, rows, all_cols = stacked.shape
    return pl.BlockSpec((None, rows, cols or all_cols), lambda *_: (layer, 0, col_block),
                        pipeline_mode=pl.Buffered(1))


def _inproj_kernel(seq, x_ref, xp_ref, xn_ref, wqk_ref, wv_ref, wo_ref, waq_ref, wakv_ref, wg_ref,
                   gb_ref, cw_ref, cb_ref, qt_ref, vt_ref, u_ref, gt_ref, win_ref):
    tm = x_ref.shape[0]
    halo = F32_SUBLANES
    tiles_per_seq = seq // tm
    pos = pl.program_id(0) % tiles_per_seq
    xe = jnp.concatenate([x_ref[...], xp_ref[...], xn_ref[...]], axis=0).astype(BF16)
    xb = xe[:tm]
    cw = cw_ref[...]
    cb = cb_ref[...]

    def plain(dst, w_ref, w_cols):
        def run():
            dst[...] = jnp.dot(xb, w_ref[:, w_cols], preferred_element_type=F32).astype(dst.dtype)
        return run

    def project(slot, c0):
        cols = slice(c0, c0 + INPROJ_COLS)
        win = win_ref.at[slot]
        pre = jnp.dot(xe, wqk_ref[:, cols], preferred_element_type=F32)
        win[0:halo, :] = jnp.where(pos > 0, pre[tm:tm + halo], 0.0)
        win[halo:halo + tm, :] = pre[:tm]
        win[halo + tm:, :] = jnp.where(pos < tiles_per_seq - 1, pre[tm + halo:], 0.0)

    def taps(slot, c0):
        cols = slice(c0, c0 + INPROJ_COLS)
        win = win_ref.at[slot]
        for r0 in range(0, tm, INPROJ_STRIP):
            rows = slice(r0, r0 + INPROJ_STRIP)
            y = cb[:, cols]
            for k in range(M_CONV):
                first = r0 + halo + k - M_CONV // 2
                y = y + cw[k:k + 1, cols] * win[first:first + INPROJ_STRIP, :]
            y = y * _sigmoid(y)
            if c0 < M_WIDTH:
                qt_ref[cols, rows] = y.T.astype(BF16)
            else:
                u_ref[rows, U_K + c0 - M_WIDTH:U_K + c0 - M_WIDTH + INPROJ_COLS] = (
                    y * (M_HEAD_DIM ** -0.5)).astype(BF16)

    def v_cols(slot, c0):
        cols = slice(c0, c0 + INPROJ_COLS)
        win = win_ref.at[slot]
        win[0:tm, :] = jnp.dot(xb, wv_ref[:, cols], preferred_element_type=F32)
        for r0 in range(0, tm, INPROJ_STRIP):
            rows = slice(r0, r0 + INPROJ_STRIP)
            vt = win[rows, :].T.astype(BF16)
            for c in range(0, INPROJ_COLS, M_HEAD_DIM):
                base = (c0 + c) // M_HEAD_DIM * M_VT_ROWS
                vt_ref[base:base + M_HEAD_DIM, rows] = vt[c:c + M_HEAD_DIM]

    ext = (lax.broadcasted_iota(jnp.int32, (M_EXT, tm), 0) == 0).astype(BF16)
    for head in range(M_HEADS):
        vt_ref[head * M_VT_ROWS + M_HEAD_DIM:(head + 1) * M_VT_ROWS, :] = ext

    def gates():
        g = jnp.dot(xb, wg_ref[...], preferred_element_type=F32) + gb_ref[...]
        for r0 in range(0, tm, INPROJ_STRIP):
            rows = slice(r0, r0 + INPROJ_STRIP)
            gt_ref[:, rows] = g[rows].T[:N_GATE]

    def conv_cols(slot, c0):
        project(slot, c0)
        taps(slot, c0)

    half = (U_AQ - U_O) // 2
    plains = [plain(u_ref.at[:, U_O:U_O + half], wo_ref, slice(0, half)),
              plain(u_ref.at[:, U_O + half:U_AQ], wo_ref, slice(half, 2 * half)),
              plain(u_ref.at[:, U_AQ:U_AQ + half], waq_ref, slice(0, half)),
              plain(u_ref.at[:, U_AQ + half:U_AKV], waq_ref, slice(half, 2 * half)),
              plain(u_ref.at[:, U_AKV:], wakv_ref, slice(None)),
              gates]
    passes = ([functools.partial(conv_cols, i % 2, c0)
               for i, c0 in enumerate(range(0, 2 * M_WIDTH, INPROJ_COLS))]
              + [functools.partial(v_cols, i % 2, c0)
                 for i, c0 in enumerate(range(0, M_WIDTH, INPROJ_COLS))])
    assert len(passes) == len(plains)
    for run_pass, run_plain in zip(passes, plains):
        run_pass()
        run_plain()


def _inproj(layer, x2, seq, wmain, waq, wakv, wg, gate_bias, conv_w, conv_b):
    t, d = x2.shape
    tm = INPROJ_TM
    halo = F32_SUBLANES
    tail = [waq, wakv, wg, gate_bias, conv_w, conv_b]
    return pl.pallas_call(
        functools.partial(_inproj_kernel, seq),
        grid=(t // tm,),
        in_specs=[pl.BlockSpec((tm, d), lambda i: (i, 0)),
                  pl.BlockSpec((halo, d), lambda i: (jnp.maximum(i * (tm // halo) - 1, 0), 0)),
                  pl.BlockSpec((halo, d),
                               lambda i: (jnp.minimum((i + 1) * (tm // halo), t // halo - 1), 0)),
                  _layer_spec(wmain, layer, 2 * M_WIDTH, 0),
                  _layer_spec(wmain, layer, M_WIDTH, 2),
                  _layer_spec(wmain, layer, M_WIDTH, 3)]
                 + [_layer_spec(w, layer) for w in tail],
        out_specs=[pl.BlockSpec((M_WIDTH, tm), lambda i: (0, i)),
                   pl.BlockSpec((M_HEADS * M_VT_ROWS, tm), lambda i: (0, i)),
                   pl.BlockSpec((tm, U_WIDTH), lambda i: (i, 0)),
                   pl.BlockSpec((N_GATE, tm), lambda i: (0, i))],
        out_shape=[jax.ShapeDtypeStruct((M_WIDTH, t), BF16),
                   jax.ShapeDtypeStruct((M_HEADS * M_VT_ROWS, t), BF16),
                   jax.ShapeDtypeStruct((t, U_WIDTH), BF16),
                   jax.ShapeDtypeStruct((N_GATE, t), F32)],
        scratch_shapes=[pltpu.VMEM((2, tm + 2 * halo, INPROJ_COLS), F32)],
        compiler_params=pltpu.CompilerParams(dimension_semantics=("arbitrary",),
                                             vmem_limit_bytes=VMEM_LIMIT),
        name="inproj",
    )(x2, x2, x2, wmain, wmain, wmain, *tail)


def _gateprep_kernel(gt_ref, rows_ref, mp_ref):
    s = gt_ref.shape[1]
    nc = s // CHUNK
    nd = 2 * M_HEADS
    gt = gt_ref[...]
    ig = gt[0:nd]
    fg = gt[nd:2 * nd]
    lf = jnp.minimum(fg, 0.0) - jnp.log1p(jnp.exp(-jnp.abs(fg)))
    pos = lax.broadcasted_iota(jnp.int32, (nd, s), 1) & (CHUNK - 1)
    fwd_row = lax.broadcasted_iota(jnp.int32, (nd, s), 0) < M_HEADS

    def chunk_scans(x, op, identity):
        pre, suf = x, x
        sh = 1
        while sh < CHUNK:
            pre = op(pre, jnp.where(pos >= sh, pltpu.roll(pre, sh, 1), identity))
            suf = op(suf, jnp.where(pos < CHUNK - sh, pltpu.roll(suf, s - sh, 1), identity))
            sh *= 2
        return pre, suf

    lf_pre, lf_suf = chunk_scans(lf, jnp.add, 0.0)
    bcum = jnp.where(fwd_row, lf_pre, lf_suf)
    gtot = lf_pre + lf_suf - lf
    r = ig - bcum
    r_pre, r_suf = chunk_scans(r, jnp.maximum, -jnp.inf)
    rcummax = jnp.where(fwd_row, r_pre, r_suf)
    rmax = jnp.maximum(r_pre, r_suf)
    m = jnp.zeros((M_HEADS, CHUNK), F32)
    for c in range(nc):
        sl = slice(c * CHUNK, (c + 1) * CHUNK)
        mp_ref[0:M_HEADS, sl] = m
        m = gtot[0:M_HEADS, sl] + jnp.maximum(m, rmax[0:M_HEADS, sl])
    m = jnp.zeros((M_HEADS, CHUNK), F32)
    for c in reversed(range(nc)):
        sl = slice(c * CHUNK, (c + 1) * CHUNK)
        mp_ref[M_HEADS:nd, sl] = m
        m = gtot[M_HEADS:nd, sl] + jnp.maximum(m, rmax[M_HEADS:nd, sl])
    m_in = mp_ref[...]
    mm = jnp.maximum(m_in, rmax)
    mu = jnp.maximum(rcummax, m_in)
    table = (r, mu, bcum + mu, jnp.exp(m_in - mu), jnp.exp(r - mm), jnp.exp(m_in - mm))
    pad = jnp.zeros((GATE_ROWS - 2 * len(table), s), F32)
    for h in range(M_HEADS):
        pieces = []
        for arr in table:
            pieces += [arr[h:h + 1], arr[M_HEADS + h:M_HEADS + h + 1]]
        rows_ref[h] = jnp.concatenate(pieces + [pad], axis=0)


def _gateprep(gt, batch, seq):
    return pl.pallas_call(
        _gateprep_kernel,
        grid=(batch,),
        in_specs=[pl.BlockSpec((N_GATE, seq), lambda b: (0, b))],
        out_specs=pl.BlockSpec((None, M_HEADS, GATE_ROWS, seq), lambda b: (b, 0, 0, 0)),
        out_shape=jax.ShapeDtypeStruct((batch, M_HEADS, GATE_ROWS, seq), F32),
        scratch_shapes=[pltpu.VMEM((2 * M_HEADS, seq), F32)],
        compiler_params=pltpu.CompilerParams(dimension_semantics=("arbitrary",),
                                             vmem_limit_bytes=VMEM_LIMIT),
        name="gateprep",
    )(gt)


def _mlstm_kernel(qt_ref, vt_ref, ks_ref, o_ref, rows_ref, ng_ref, out_ref,
                  csf_ref, csb_ref, cf_ref, cb_ref):
    seq = ks_ref.shape[0]
    nc = seq // CHUNK
    dh = M_HEAD_DIM

    def gate_row(kind, rev, r0):
        return rows_ref[2 * kind + rev:2 * kind + rev + 1, pl.ds(r0, CHUNK)]

    def state_step(ci, rev, c_ref, cs_ref):
        r0 = pl.multiple_of(ci * CHUNK, CHUNK)
        cs_ref[ci] = c_ref[...].astype(BF16)
        vw = (vt_ref[:, pl.ds(r0, CHUNK)].astype(F32) * gate_row(4, rev, r0)).astype(BF16)
        decay = gate_row(5, rev, r0)[:, 0:1]
        c_ref[...] = decay * c_ref[...] + jnp.dot(vw, ks_ref[pl.ds(r0, CHUNK), :],
                                                  preferred_element_type=F32)

    cf_ref[...] = jnp.zeros_like(cf_ref)
    cb_ref[...] = jnp.zeros_like(cb_ref)

    def state_body(c, carry):
        state_step(c, 0, cf_ref, csf_ref)
        state_step(nc - 1 - c, 1, cb_ref, csb_ref)
        return carry

    lax.fori_loop(0, nc, state_body, 0, unroll=4)

    s_idx = lax.broadcasted_iota(jnp.int32, (LANES, LANES), 0)
    t_idx = lax.broadcasted_iota(jnp.int32, (LANES, LANES), 1)
    tri = (s_idx <= t_idx, s_idx >= t_idx)
    krow = lax.broadcasted_iota(jnp.int32, (BF16_SUBLANES, CHUNK), 0)
    nblk = CHUNK // LANES

    def intra_weights(a, qk, rev):
        cols = []
        for bt in range(nblk):
            col = []
            for bs in range(nblk):
                blk = (slice(bs * LANES, (bs + 1) * LANES), slice(bt * LANES, (bt + 1) * LANES))
                if bs == bt:
                    w = qk[blk] * jnp.exp(jnp.where(tri[rev], a[blk], -jnp.inf))
                elif (bs > bt) == bool(rev):
                    w = qk[blk] * jnp.exp(a[blk])
                else:
                    w = jnp.zeros((LANES, LANES), F32)
                col.append(w.astype(BF16))
            cols.append(jnp.concatenate(col, axis=0))
        return jnp.concatenate(cols, axis=1)

    def split_rows(x, first):
        hi = x.astype(BF16).astype(F32)
        mid = (x - hi).astype(BF16).astype(F32)
        lo = x - hi - mid
        out = jnp.where(krow == first, hi, jnp.where(krow == first + 1, mid,
                                                     jnp.where(krow == first + 2, lo, 0.0)))
        ones = (krow >= 3 - first) & (krow < 6 - first)
        return jnp.where(ones, 1.0, out).astype(BF16)

    dirs = ((0, csf_ref), (1, csb_ref))

    def out_body(g, carry):
        cs = [g * M_OUT_GROUP + i for i in range(M_OUT_GROUP)]
        r0s = [pl.multiple_of(c * CHUNK, CHUNK) for c in cs]
        qts = [qt_ref[:, pl.ds(r0, CHUNK)] for r0 in r0s]
        qks = [jnp.dot(ks_ref[pl.ds(r0, CHUNK), :], qt, preferred_element_type=F32)
               for r0, qt in zip(r0s, qts)]
        a_s = [[lax.dot_general(split_rows(gate_row(0, rev, r0), 0),
                                split_rows(-gate_row(1, rev, r0), 3),
                                (((0,), (0,)), ((), ())), preferred_element_type=F32)
                for rev, _ in dirs] for r0 in r0s]
        inters = [[jnp.dot(cs_ref[c], qt, preferred_element_type=F32) for _, cs_ref in dirs]
                  for c, qt in zip(cs, qts)]
        ws = [[intra_weights(a[rev], qk, rev) for rev, _ in dirs] for a, qk in zip(a_s, qks)]
        tots = [[jnp.dot(vt_ref[:, pl.ds(r0, CHUNK)], w[rev], preferred_element_type=F32)
                 + gate_row(3, rev, r0) * inter[rev] for rev, _ in dirs]
                for r0, w, inter in zip(r0s, ws, inters)]
        for r0, tot in zip(r0s, tots):
            h = None
            for rev, _ in dirs:
                den = jnp.maximum(jnp.abs(tot[rev][dh:dh + 1, :]), jnp.exp(-gate_row(2, rev, r0)))
                hd = tot[rev][:dh, :] * (1.0 / den)
                h = hd if h is None else h + hd
            hc = h - jnp.mean(h, axis=0, keepdims=True)
            hn = hc * lax.rsqrt(jnp.mean(hc * hc, axis=0, keepdims=True) + MH_EPS)
            og = _sigmoid(o_ref[pl.ds(r0, CHUNK), :].astype(F32))
            out_ref[pl.ds(r0, CHUNK), :] = (hn.T * ng_ref[...] * og).astype(BF16)
        return carry

    lax.fori_loop(0, nc // M_OUT_GROUP, out_body, 0)


def _mlstm(layer, qt, vt, u, rows, norm_g, batch, seq):
    dh = M_HEAD_DIM
    de = M_VT_ROWS
    nc = seq // CHUNK

    def tcol(rows):
        return pl.BlockSpec((rows, seq), lambda b, h: (h, b))

    def ucol(base):
        return pl.BlockSpec((seq, dh), lambda b, h: (b, base // dh + h))

    return pl.pallas_call(
        _mlstm_kernel,
        grid=(batch, M_HEADS),
        in_specs=[tcol(dh), tcol(de), ucol(U_K), ucol(U_O),
                  pl.BlockSpec((None, None, GATE_ROWS, seq), lambda b, h: (b, h, 0, 0)),
                  pl.BlockSpec((None, 1, dh), lambda b, h: (layer, 0, h))],
        out_specs=pl.BlockSpec((seq, dh), lambda b, h: (b, h)),
        out_shape=jax.ShapeDtypeStruct((batch * seq, M_WIDTH), BF16),
        scratch_shapes=[pltpu.VMEM((nc, de, dh), BF16),
                        pltpu.VMEM((nc, de, dh), BF16),
                        pltpu.VMEM((de, dh), F32),
                        pltpu.VMEM((de, dh), F32)],
        compiler_params=pltpu.CompilerParams(dimension_semantics=("arbitrary", "arbitrary"),
                                             vmem_limit_bytes=VMEM_LIMIT),
        name="mlstm",
    )(qt, vt, u, u, rows, norm_g)


A_DELTAS = (-A_BLOCK, 0, -2 * A_BLOCK)


def _attn_kernel(layer, sink_ref, q_ref, kv_ref, out_ref, bias_ref):
    seq = q_ref.shape[0]
    nb = seq // A_BLOCK
    span = 3 * A_BLOCK
    lo = lax.broadcasted_iota(jnp.int32, (A_BLOCK, LANES), 1) < A_HEAD_DIM
    lo_kv = lax.broadcasted_iota(jnp.int32, (span, LANES), 1) < A_HEAD_DIM
    sum_lane = lax.broadcasted_iota(jnp.int32, (span, LANES), 1)
    qi = lax.broadcasted_iota(jnp.int32, (A_BLOCK, span), 0)
    ki = lax.broadcasted_iota(jnp.int32, (A_BLOCK, span), 1)
    for variant, delta in enumerate(A_DELTAS):
        rel = jnp.abs(ki - qi + delta)
        base = jnp.where(rel <= WINDOW, -rel.astype(F32), -jnp.inf)
        for j in range(A_GROUP):
            for half in range(A_KV_HEADS):
                head = j + A_GROUP * half
                bias_ref[variant * A_GROUP + j, :, half * span:(half + 1) * span] = (
                    base * (2.0 ** -(head + 1)))

    def blocks(i, carry):
        q0s, kds, vds, variants = [], [], [], []
        for b in range(A_BLOCKS_PER_ITER):
            n = i * A_BLOCKS_PER_ITER + b
            q0 = pl.multiple_of(n * A_BLOCK, A_BLOCK)
            k0 = pl.multiple_of(jnp.clip(q0 - A_BLOCK, 0, seq - span), A_BLOCK)
            k3 = kv_ref[pl.ds(k0, span), 0:A_KV_WIDTH]
            v3 = kv_ref[pl.ds(k0, span), A_KV_WIDTH:2 * A_KV_WIDTH]
            zero = jnp.zeros_like(k3)
            q0s.append(q0)
            variants.append(jnp.where(n == 0, 1, jnp.where(n == nb - 1, 2, 0)))
            kds.append(jnp.concatenate([jnp.where(lo_kv, k3, zero), jnp.where(lo_kv, zero, k3)],
                                       axis=0))
            vds.append(jnp.concatenate(
                [jnp.concatenate([jnp.where(lo_kv, v3, zero), (sum_lane == 0).astype(BF16)], axis=1),
                 jnp.concatenate([jnp.where(lo_kv, zero, v3), (sum_lane == 1).astype(BF16)], axis=1)],
                axis=0))
        work = [(b, j) for b in range(A_BLOCKS_PER_ITER) for j in range(A_GROUP)]
        scs = [lax.dot_general(q_ref[pl.ds(q0s[b], A_BLOCK), j * LANES:(j + 1) * LANES], kds[b],
                               (((1,), (1,)), ((), ())), preferred_element_type=F32)
               + bias_ref[variants[b] * A_GROUP + j] for b, j in work]
        es, corrs = [], []
        for (b, j), sc in zip(work, scs):
            e, corr = [], []
            for half in range(A_KV_HEADS):
                sk = sink_ref[layer, j + A_GROUP * half]
                sch = sc[:, half * span:(half + 1) * span]
                mx = jnp.maximum(jnp.max(sch, axis=1, keepdims=True), sk)
                e.append(jnp.exp(sch - mx).astype(BF16))
                corr.append(jnp.exp(sk - mx))
            es.append(jnp.concatenate(e, axis=1))
            corrs.append(corr)
        pvs = [jnp.dot(e, vds[b], preferred_element_type=F32) for (b, j), e in zip(work, es)]
        for (b, j), pv, corr in zip(work, pvs, corrs):
            rden = [1.0 / (pv[:, LANES + half:LANES + half + 1] + corr[half])
                    for half in range(A_KV_HEADS)]
            out_ref[pl.ds(q0s[b], A_BLOCK), j * LANES:(j + 1) * LANES] = (
                pv[:, :LANES] * jnp.where(lo, rden[0], rden[1])).astype(BF16)
        return carry

    lax.fori_loop(0, nb // A_BLOCKS_PER_ITER, blocks, 0)


def _attn(layer, u, sink, batch, seq):
    return pl.pallas_call(
        functools.partial(_attn_kernel, layer),
        grid=(batch,),
        in_specs=[pl.BlockSpec(memory_space=pltpu.SMEM),
                  pl.BlockSpec((seq, A_WIDTH), lambda b: (b, U_AQ // A_WIDTH)),
                  pl.BlockSpec((seq, 2 * A_KV_WIDTH), lambda b: (b, U_AKV // (2 * A_KV_WIDTH)))],
        out_specs=pl.BlockSpec((seq, A_WIDTH), lambda b: (b, 0)),
        out_shape=jax.ShapeDtypeStruct((batch * seq, A_WIDTH), BF16),
        scratch_shapes=[pltpu.VMEM((len(A_DELTAS) * A_GROUP, A_BLOCK, 6 * A_BLOCK), F32)],
        compiler_params=pltpu.CompilerParams(dimension_semantics=("arbitrary",),
                                             vmem_limit_bytes=VMEM_LIMIT),
        name="attn",
    )(sink, u, u)


def _ff_chunks(d_ff):
    step = 1024
    return [(c0, min(c0 + step, d_ff)) for c0 in range(0, d_ff, step)]


def _post_kernel(alpha, x_ref, hm_ref, ha_ref, p_ref, wom_ref, woa_ref, wg_ref, wu_ref, wd_ref,
                 wpg_ref, wpp_ref, ln1g_ref, ln1b_ref, ln2g_ref, ln2b_ref, y_ref):
    subs = [slice(r0, r0 + POST_SUB) for r0 in range(0, x_ref.shape[0], POST_SUB)]
    n = len(subs)
    x1s, accs = {}, {}

    def mix_ln1(i):
        rows = subs[i]
        mix = jnp.dot(hm_ref[rows, :], wom_ref[...], preferred_element_type=F32)
        mix = mix + jnp.dot(ha_ref[rows, :], woa_ref[...], preferred_element_type=F32)
        x1s[i] = _layer_norm(alpha * x_ref[rows, :] + mix, ln1g_ref[...], ln1b_ref[...])

    def ffn_ple(i):
        x1 = x1s.pop(i)
        x1b = x1.astype(BF16)
        ple_gate = _sigmoid(jnp.dot(x1b, wpg_ref[...], preferred_element_type=F32))
        acc = alpha * x1 + ple_gate * jnp.dot(p_ref[subs[i], :].astype(BF16), wpp_ref[...],
                                              preferred_element_type=F32)
        for c0, c1 in _ff_chunks(wd_ref.shape[0]):
            gate = jnp.dot(x1b, wg_ref[:, c0:c1], preferred_element_type=F32)
            up = jnp.dot(x1b, wu_ref[:, c0:c1], preferred_element_type=F32)
            hid = (gate * _sigmoid(gate) * up).astype(BF16)
            acc = acc + jnp.dot(hid, wd_ref[c0:c1, :], preferred_element_type=F32)
        accs[i] = acc

    def ln2(i):
        y_ref[subs[i], :] = _layer_norm(accs.pop(i), ln2g_ref[...], ln2b_ref[...])

    for i in range(min(2, n)):
        mix_ln1(i)
    for i in range(n):
        ffn_ple(i)
        if i + 2 < n:
            mix_ln1(i + 2)
        if i >= 1:
            ln2(i - 1)
    ln2(n - 1)


def _post(layer, alpha, x2, hm, ha, p3, wom, woa, wgu, wd, wpg, wpp, ln1g, ln1b, ln2g, ln2b):
    t, d = x2.shape
    tm = POST_TM
    d_ff = wd.shape[1]

    def rows(width):
        return pl.BlockSpec((tm, width), lambda i: (i, 0))

    tail = [wd, wpg, wpp, ln1g, ln1b, ln2g, ln2b]
    return pl.pallas_call(
        functools.partial(_post_kernel, alpha),
        grid=(t // tm,),
        in_specs=[rows(d), rows(hm.shape[1]), rows(ha.shape[1]),
                  pl.BlockSpec((None, tm, p3.shape[2]), lambda i: (layer, i, 0)),
                  _layer_spec(wom, layer), _layer_spec(woa, layer),
                  _layer_spec(wgu, layer, d_ff, 0), _layer_spec(wgu, layer, d_ff, 1)]
                 + [_layer_spec(w, layer) for w in tail],
        out_specs=rows(d),
        out_shape=jax.ShapeDtypeStruct((t, d), F32),
        compiler_params=pltpu.CompilerParams(dimension_semantics=("arbitrary",),
                                             vmem_limit_bytes=VMEM_LIMIT),
        name="post",
    )(x2, hm, ha, p3, wom, woa, wgu, wgu, *tail)


def _pair_heads(w, axis):
    shape = w.shape
    w = w.reshape(shape[:axis] + (A_KV_HEADS, A_GROUP, A_HEAD_DIM) + shape[axis + 1:])
    return jnp.swapaxes(w, axis, axis + 1).reshape(shape)


def kernel(x, p, w_in, b_gate, conv_w, conv_b, mlstm_norm_g, attn_sink, w_out, ln1_g, ln1_b,
           w_ffn_in, w_ffn_out, ln2_g, ln2_b, w_ple_gate, w_ple_proj):
    batch, seq, d = x.shape
    depth = w_in.shape[0]
    t = batch * seq
    alpha = float((2 * depth) ** 0.25)
    assert seq % (CHUNK * M_OUT_GROUP) == 0 and seq >= 3 * A_BLOCK
    assert seq % (A_BLOCK * A_BLOCKS_PER_ITER) == 0
    assert t % INPROJ_TM == 0 and t % POST_TM == 0
    assert seq % INPROJ_TM == 0
    assert w_in.shape[2] == 4 * M_WIDTH + N_GATE + A_WIDTH + 2 * A_KV_WIDTH

    g0 = 4 * M_WIDTH
    aq0 = g0 + N_GATE
    akv0 = aq0 + A_WIDTH
    assert A_HEAD_DIM in (4 ** e for e in range(8)), "softmax scale must be a power of two to fold"
    w_in16 = w_in.astype(BF16)
    waq = _pair_heads(w_in16[:, :, aq0:akv0], 2) * jnp.asarray(A_HEAD_DIM ** -0.5, BF16)
    wakv = w_in16[:, :, akv0:]
    wgate = jnp.pad(w_in16[:, :, g0:aq0], ((0, 0), (0, 0), (0, G_WIDTH - N_GATE)))
    gate_bias = jnp.pad(b_gate, ((0, 0), (0, G_WIDTH - N_GATE)))[:, None, :]
    wom = w_out[:, :M_WIDTH].astype(BF16)
    woa = _pair_heads(w_out[:, M_WIDTH:], 1).astype(BF16)
    wgu = w_ffn_in.astype(BF16)
    wd = w_ffn_out.astype(BF16)
    wpg = w_ple_gate.astype(BF16)
    wpp = w_ple_proj.astype(BF16)
    conv_b3, norm_g3, ln1_g3, ln1_b3, ln2_g3, ln2_b3 = (
        a[:, None, :] for a in (conv_b, mlstm_norm_g, ln1_g, ln1_b, ln2_g, ln2_b))

    h = x.reshape(t, d)
    p3 = p.reshape(depth, t, p.shape[-1])
    for i in range(depth):
        qt, vt, u, gt = _inproj(i, h, seq, w_in16, waq, wakv, wgate, gate_bias, conv_w, conv_b3)
        rows = _gateprep(gt, batch, seq)
        hm = _mlstm(i, qt, vt, u, rows, norm_g3, batch, seq)
        ha = _attn(i, u, attn_sink, batch, seq)
        h = _post(i, alpha, h, hm, ha, p3, wom, woa, wgu, wd, wpg, wpp,
                  ln1_g3, ln1_b3, ln2_g3, ln2_b3)
    return h.reshape(batch, seq, d)
```
